```python
import jax, jax.numpy as jnp
from jax import lax
import numpy as np

D_MODEL = 4096
BATCH = 4
SEQ = 4096
DEPTH = 1
DEC_BATCH = 1
DEC_SEQ = 8192
PAST_LEN = 128

HEAD_DIM = 128
N_Q_HEADS = 16
N_KV_HEADS = 4
GQA_GROUP = N_Q_HEADS // N_KV_HEADS
ATTN_WIDTH = N_Q_HEADS * HEAD_DIM
KV_WIDTH = N_KV_HEADS * HEAD_DIM
Q_BLOCK = 128
ROPE_THETA = 10000.0
AXIS_DIM = HEAD_DIM // 2
GRID_W = 64
D_RNN = D_MODEL // 2
LRU_BLOCKS = 16
LRU_BLOCK_W = D_RNN // LRU_BLOCKS
LRU_C = 8.0
CONV_W = 4
CONV_PAD_LO = 1
D_FF = 4 * D_MODEL
NORM_EPS = 1e-6
IN_WIDTH = ATTN_WIDTH + 2 * KV_WIDTH + 2 * D_RNN + 2 * D_MODEL

kernel_name = "hybrid_gated_rglru_axial_gqa_encoder"


def _rmsnorm(x, g):
    xf = x.astype(jnp.float32)
    y = xf * lax.rsqrt(jnp.mean(xf * xf, axis=-1, keepdims=True) + NORM_EPS)
    return (y * g.astype(jnp.float32)).astype(x.dtype)


def _axial_rope_tables(seq_len):
    n_rows = seq_len // GRID_W
    row = jnp.repeat(jnp.arange(n_rows), GRID_W).astype(jnp.float32)
    col = jnp.tile(jnp.arange(GRID_W), n_rows).astype(jnp.float32)
    inv = ROPE_THETA ** (-jnp.arange(0, AXIS_DIM, 2, dtype=jnp.float32) / AXIS_DIM)
    ang = jnp.concatenate([row[:, None] * inv, col[:, None] * inv], axis=-1)
    return jnp.cos(ang), jnp.sin(ang)


def _apply_rope(x, cos, sin):
    B, S, H, _ = x.shape
    xf = x.astype(jnp.float32).reshape(B, S, H, HEAD_DIM // 2, 2)
    c = cos[None, :, None, :]
    s = sin[None, :, None, :]
    x1, x2 = xf[..., 0], xf[..., 1]
    out = jnp.stack([x1 * c - x2 * s, x1 * s + x2 * c], axis=-1)
    return out.reshape(B, S, H, HEAD_DIM).astype(x.dtype)


def _block_attention(q, k, v):
    B, S = q.shape[:2]
    nblk = S // Q_BLOCK
    qb = q.reshape(B, nblk, Q_BLOCK, N_KV_HEADS, GQA_GROUP, HEAD_DIM).transpose(1, 0, 3, 4, 2, 5)
    scale = HEAD_DIM ** -0.5

    def one_block(qblk):
        s = jnp.einsum('bkgqd,bskd->bkgqs', qblk, k).astype(jnp.float32) * scale
        p = jax.nn.softmax(s, axis=-1).astype(v.dtype)
        return jnp.einsum('bkgqs,bskd->bkgqd', p, v)

    o = lax.map(one_block, qb)
    return o.transpose(1, 0, 4, 2, 3, 5).reshape(B, S, ATTN_WIDTH)


def _centred_dwconv(x, w, b):
    S = x.shape[1]
    xp = jnp.pad(x, ((0, 0), (CONV_PAD_LO, CONV_W - 1 - CONV_PAD_LO), (0, 0)))
    out = b
    for j in range(CONV_W):
        out = out + w[j] * xp[:, j:j + S]
    return out


def _lru_combine(left, right):
    a1, b1 = left
    a2, b2 = right
    return a1 * a2, a2 * b1 + b2


def _bidir_rg_lru(xc, w_a, b_a, w_i, b_i, lam):
    B, S, _ = xc.shape
    xf = xc.astype(jnp.float32)
    xb = xf.reshape(B, S, LRU_BLOCKS, LRU_BLOCK_W)
    r = jax.nn.sigmoid(jnp.einsum('bsnc,zncm->zbsnm', xb, w_a.astype(jnp.float32)).reshape(2, B, S, D_RNN)
                       + b_a.astype(jnp.float32)[:, None, None, :])
    i = jax.nn.sigmoid(jnp.einsum('bsnc,zncm->zbsnm', xb, w_i.astype(jnp.float32)).reshape(2, B, S, D_RNN)
                       + b_i.astype(jnp.float32)[:, None, None, :])
    log_a = -LRU_C * r * jax.nn.softplus(-lam.astype(jnp.float32))[:, None, None, :]
    a = jnp.exp(log_a)
    u = jnp.sqrt(-jnp.expm1(2.0 * log_a)) * (i * xf[None])
    _, h_fwd = lax.associative_scan(_lru_combine, (a[0], u[0]), axis=1)
    _, h_bwd = lax.associative_scan(_lru_combine, (a[1], u[1]), axis=1, reverse=True)
    return (h_fwd + h_bwd).astype(xc.dtype)


def _encoder_layer(x, norm_mix, w_in, q_norm, k_norm, conv_w, conv_b, lru_w_a, lru_b_a,
                   lru_w_i, lru_b_i, lru_lambda, w_attn_out, w_rnn_out, b_gate, w_out,
                   norm_mlp, w_up, w_down):
    B, S, _ = x.shape
    h = _rmsnorm(x, norm_mix)
    z = h @ w_in
    o1 = ATTN_WIDTH
    o2 = o1 + KV_WIDTH
    o3 = o2 + KV_WIDTH
    o4 = o3 + D_RNN
    o5 = o4 + D_RNN
    q, k, v, xr, yr, gl = jnp.split(z, [o1, o2, o3, o4, o5], axis=-1)

    cos, sin = _axial_rope_tables(S)
    q = _apply_rope(_rmsnorm(q.reshape(B, S, N_Q_HEADS, HEAD_DIM), q_norm), cos, sin)
    k = _apply_rope(_rmsnorm(k.reshape(B, S, N_KV_HEADS, HEAD_DIM), k_norm), cos, sin)
    v = v.reshape(B, S, N_KV_HEADS, HEAD_DIM)
    attn_branch = _block_attention(q, k, v) @ w_attn_out

    xc = _centred_dwconv(xr, conv_w, conv_b)
    rec = _bidir_rg_lru(xc, lru_w_a, lru_b_a, lru_w_i, lru_b_i, lru_lambda) * jax.nn.gelu(yr)
    rnn_branch = rec @ w_rnn_out

    gates = jax.nn.sigmoid(gl.reshape(B, S, 2, D_MODEL) + b_gate)
    merged = gates[:, :, 0] * attn_branch + gates[:, :, 1] * rnn_branch
    x = x + merged @ w_out

    hm = _rmsnorm(x, norm_mlp)
    x = x + jnp.square(jax.nn.relu(hm @ w_up)) @ w_down
    return x


def _trunk(x, norm_mix, w_in, q_norm, k_norm, conv_w, conv_b, lru_w_a, lru_b_a, lru_w_i,
           lru_b_i, lru_lambda, w_attn_out, w_rnn_out, b_gate, w_out, norm_mlp, w_up,
           w_down, norm_final):
    for l in range(DEPTH):
        x = _encoder_layer(x, norm_mix[l], w_in[l], q_norm[l], k_norm[l], conv_w[l], conv_b[l],
                           lru_w_a[l], lru_b_a[l], lru_w_i[l], lru_b_i[l], lru_lambda[l],
                           w_attn_out[l], w_rnn_out[l], b_gate[l], w_out[l], norm_mlp[l],
                           w_up[l], w_down[l])
    return _rmsnorm(x, norm_final)


def setup_inputs(seed: int = 0) -> dict:
    key = jax.random.key(seed)
    ks = jax.random.split(key, 24)
    f32 = jnp.float32

    def nrm(k, shape, fan_in):
        return jax.random.normal(k, shape, f32) * (fan_in ** -0.5)

    def gain(k, shape):
        return 1.0 + 0.02 * jax.random.normal(k, shape, f32)

    def small(k, shape):
        return 0.01 * jax.random.normal(k, shape, f32)

    u = jax.random.uniform(ks[13], (DEPTH, 2, D_RNN), f32, minval=0.9, maxval=0.999)
    a0 = u ** (1.0 / LRU_C)
    lru_lambda = jnp.log(a0) - jnp.log1p(-a0)
    return {
        "x_prompt": jax.random.normal(ks[0], (BATCH, SEQ, D_MODEL), f32),
        "x_sample": jax.random.normal(ks[1], (DEC_BATCH, DEC_SEQ, D_MODEL), f32),
        "norm_mix": gain(ks[2], (DEPTH, D_MODEL)),
        "w_in": nrm(ks[3], (DEPTH, D_MODEL, IN_WIDTH), D_MODEL),
        "q_norm": gain(ks[4], (DEPTH, HEAD_DIM)),
        "k_norm": gain(ks[5], (DEPTH, HEAD_DIM)),
        "conv_w": nrm(ks[6], (DEPTH, CONV_W, D_RNN), CONV_W),
        "conv_b": small(ks[7], (DEPTH, D_RNN)),
        "lru_w_a": nrm(ks[8], (DEPTH, 2, LRU_BLOCKS, LRU_BLOCK_W, LRU_BLOCK_W), LRU_BLOCK_W),
        "lru_b_a": small(ks[9], (DEPTH, 2, D_RNN)),
        "lru_w_i": nrm(ks[10], (DEPTH, 2, LRU_BLOCKS, LRU_BLOCK_W, LRU_BLOCK_W), LRU_BLOCK_W),
        "lru_b_i": small(ks[11], (DEPTH, 2, D_RNN)),
        "lru_lambda": lru_lambda,
        "w_attn_out": nrm(ks[14], (DEPTH, ATTN_WIDTH, D_MODEL), ATTN_WIDTH),
        "w_rnn_out": nrm(ks[15], (DEPTH, D_RNN, D_MODEL), D_RNN),
        "b_gate": small(ks[16], (DEPTH, 2, D_MODEL)),
        "w_out": nrm(ks[17], (DEPTH, D_MODEL, D_MODEL), D_MODEL),
        "norm_mlp": gain(ks[18], (DEPTH, D_MODEL)),
        "w_up": nrm(ks[19], (DEPTH, D_MODEL, D_FF), D_MODEL),
        "w_down": nrm(ks[20], (DEPTH, D_FF, D_MODEL), D_FF),
        "norm_final": gain(ks[21], (D_MODEL,)),
    }


def reference(x_prompt, x_sample, norm_mix, w_in, q_norm, k_norm, conv_w, conv_b, lru_w_a,
              lru_b_a, lru_w_i, lru_b_i, lru_lambda, w_attn_out, w_rnn_out, b_gate, w_out,
              norm_mlp, w_up, w_down, norm_final):
    y_prompt = _trunk(x_prompt, norm_mix, w_in, q_norm, k_norm, conv_w, conv_b, lru_w_a, lru_b_a,
                      lru_w_i, lru_b_i, lru_lambda, w_attn_out, w_rnn_out, b_gate, w_out,
                      norm_mlp, w_up, w_down, norm_final)
    y_sample = _trunk(x_sample, norm_mix, w_in, q_norm, k_norm, conv_w, conv_b, lru_w_a, lru_b_a,
                      lru_w_i, lru_b_i, lru_lambda, w_attn_out, w_rnn_out, b_gate, w_out,
                      norm_mlp, w_up, w_down, norm_final)
    return (y_prompt, y_sample)
```

```python
import functools
import math

import jax
import jax.numpy as jnp
from jax import lax
from jax.experimental import pallas as pl
from jax.experimental.pallas import tpu as pltpu

F32 = jnp.float32
BF16 = jnp.bfloat16

HEAD_DIM = 128
N_Q_HEADS = 16
N_KV_HEADS = 4
GQA_GROUP = N_Q_HEADS // N_KV_HEADS
ATTN_WIDTH = N_Q_HEADS * HEAD_DIM
KV_WIDTH = N_KV_HEADS * HEAD_DIM
ROPE_THETA = 10000.0
GRID_W = 64
LRU_C = 8.0
CONV_W = 4
NORM_EPS = 1e-6
LANES = 128
HALO = 16
V7X_VMEM_LIMIT = 56 * 1024 * 1024


def _params(n_axes, vmem=V7X_VMEM_LIMIT):
    return pltpu.CompilerParams(
        dimension_semantics=("arbitrary",) * n_axes, vmem_limit_bytes=vmem)


def _tile(n, want):
    t = min(n, want)
    while n % t:
        t //= 2
    return t


def _rmsnorm_cast_kernel(x_ref, g_ref, o_ref):
    x = x_ref[...]
    ms = jnp.mean(x * x, axis=-1, keepdims=True)
    o_ref[...] = (x * lax.rsqrt(ms + NORM_EPS) * g_ref[...]).astype(o_ref.dtype)


def _rmsnorm_cast(x, g):
    t, d = x.shape
    tr = _tile(t, 256)
    return pl.pallas_call(
        _rmsnorm_cast_kernel,
        grid=(t // tr,),
        in_specs=[pl.BlockSpec((tr, d), lambda i: (i, 0)),
                  pl.BlockSpec((1, d), lambda i: (0, 0))],
        out_specs=pl.BlockSpec((tr, d), lambda i: (i, 0)),
        out_shape=jax.ShapeDtypeStruct((t, d), BF16),
        compiler_params=_params(1),
        name="rmsnorm_cast",
    )(x, g.reshape(1, d))


def _mm_kernel(a_ref, w_ref, o_ref):
    o_ref[...] = jnp.dot(a_ref[...], w_ref[...],
                         preferred_element_type=F32).astype(o_ref.dtype)


def _matmul(a, w, name):
    m, k = a.shape
    n = w.shape[1]
    tm, tn = _tile(m, 1024), _tile(n, 1024)
    return pl.pallas_call(
        _mm_kernel,
        grid=(m // tm, n // tn),
        in_specs=[pl.BlockSpec((tm, k), lambda i, j: (i, 0)),
                  pl.BlockSpec((k, tn), lambda i, j: (0, j))],
        out_specs=pl.BlockSpec((tm, tn), lambda i, j: (i, j)),
        out_shape=jax.ShapeDtypeStruct((m, n), BF16),
        compiler_params=_params(2),
        name=name,
    )(a, w)


def _mlp_up_kernel(a_ref, w_ref, o_ref):
    y = jnp.dot(a_ref[...], w_ref[...], preferred_element_type=F32)
    r = jnp.maximum(y, 0.0)
    o_ref[...] = (r * r).astype(o_ref.dtype)


def _mlp_up(a, w):
    m, k = a.shape
    n = w.shape[1]
    tm, tn = _tile(m, 1024), _tile(n, 1024)
    return pl.pallas_call(
        _mlp_up_kernel,
        grid=(m // tm, n // tn),
        in_specs=[pl.BlockSpec((tm, k), lambda i, j: (i, 0)),
                  pl.BlockSpec((k, tn), lambda i, j: (0, j))],
        out_specs=pl.BlockSpec((tm, tn), lambda i, j: (i, j)),
        out_shape=jax.ShapeDtypeStruct((m, n), BF16),
        compiler_params=_params(2),
        name="mlp_up",
    )(a, w)


def _merge_kernel(attn_ref, rec_ref, wa_ref, wr_ref, ga_ref, gr_ref, ba_ref, br_ref, o_ref):
    ab = jnp.dot(attn_ref[...], wa_ref[...], preferred_element_type=F32)
    rb = jnp.dot(rec_ref[...], wr_ref[...], preferred_element_type=F32)
    ga = jax.nn.sigmoid(ga_ref[...].astype(F32) + ba_ref[...])
    gr = jax.nn.sigmoid(gr_ref[...].astype(F32) + br_ref[...])
    o_ref[...] = (ga * ab + gr * rb).astype(o_ref.dtype)


def _merge(attn, rec, wa, wr, z, b_gate, gate_col0):
    m, ka = attn.shape
    kr = rec.shape[1]
    n = wa.shape[1]
    tm, tn = _tile(m, 1024), _tile(n, 512)
    ga_blk = gate_col0 // tn
    gr_blk = (gate_col0 + n) // tn
    return pl.pallas_call(
        _merge_kernel,
        grid=(m // tm, n // tn),
        in_specs=[pl.BlockSpec((tm, ka), lambda i, j: (i, 0)),
                  pl.BlockSpec((tm, kr), lambda i, j: (i, 0)),
                  pl.BlockSpec((ka, tn), lambda i, j: (0, j)),
                  pl.BlockSpec((kr, tn), lambda i, j: (0, j)),
                  pl.BlockSpec((tm, tn), lambda i, j: (i, ga_blk + j)),
                  pl.BlockSpec((tm, tn), lambda i, j: (i, gr_blk + j)),
                  pl.BlockSpec((1, tn), lambda i, j: (0, j)),
                  pl.BlockSpec((1, tn), lambda i, j: (0, j))],
        out_specs=pl.BlockSpec((tm, tn), lambda i, j: (i, j)),
        out_shape=jax.ShapeDtypeStruct((m, n), BF16),
        compiler_params=_params(2),
        name="branch_merge",
    )(attn, rec, wa, wr, z, z, b_gate[0:1], b_gate[1:2])


def _outproj_kernel(a_ref, w_ref, x_ref, o_ref):
    o_ref[...] = x_ref[...] + jnp.dot(a_ref[...], w_ref[...], preferred_element_type=F32)


def _outproj(a, w, x):
    m, k = a.shape
    n = w.shape[1]
    tm, tn = _tile(m, 1024), _tile(n, 512)
    return pl.pallas_call(
        _outproj_kernel,
        grid=(m // tm, n // tn),
        in_specs=[pl.BlockSpec((tm, k), lambda i, j: (i, 0)),
                  pl.BlockSpec((k, tn), lambda i, j: (0, j)),
                  pl.BlockSpec((tm, tn), lambda i, j: (i, j))],
        out_specs=pl.BlockSpec((tm, tn), lambda i, j: (i, j)),
        out_shape=jax.ShapeDtypeStruct((m, n), F32),
        compiler_params=_params(2),
        name="out_proj",
    )(a, w, x)


def _mlp_down_kernel(u_ref, w_ref, x_ref, g_ref, o_ref, *, final_norm):
    kk = pl.program_id(1)

    @pl.when(kk == 0)
    def _():
        o_ref[...] = x_ref[...]

    o_ref[...] += jnp.dot(u_ref[...], w_ref[...], preferred_element_type=F32)

    if final_norm:
        @pl.when(kk == pl.num_programs(1) - 1)
        def _():
            x = o_ref[...]
            ms = jnp.mean(x * x, axis=-1, keepdims=True)
            o_ref[...] = x * lax.rsqrt(ms + NORM_EPS) * g_ref[...]


def _mlp_down(u, w, x, g, final_norm):
    m, k = u.shape
    n = w.shape[1]
    tm, tk = _tile(m, 512), _tile(k, 512)
    return pl.pallas_call(
        functools.partial(_mlp_down_kernel, final_norm=final_norm),
        grid=(m // tm, k // tk),
        in_specs=[pl.BlockSpec((tm, tk), lambda i, kk: (i, kk)),
                  pl.BlockSpec((tk, n), lambda i, kk: (kk, 0)),
                  pl.BlockSpec((tm, n), lambda i, kk: (i, 0)),
                  pl.BlockSpec((1, n), lambda i, kk: (0, 0))],
        out_specs=pl.BlockSpec((tm, n), lambda i, kk: (i, 0)),
        out_shape=jax.ShapeDtypeStruct((m, n), F32),
        compiler_params=_params(2),
        name="mlp_down_norm",
    )(u, w, x, g.reshape(1, n))


def _rope_tables(seq_len):
    axis_dim = HEAD_DIM // 2
    n_rows = seq_len // GRID_W
    row = jnp.repeat(jnp.arange(n_rows), GRID_W).astype(F32)
    col = jnp.tile(jnp.arange(GRID_W), n_rows).astype(F32)
    inv = ROPE_THETA ** (-jnp.arange(0, axis_dim, 2, dtype=F32) / axis_dim)
    ang = jnp.concatenate([row[:, None] * inv, col[:, None] * inv], axis=-1)
    cos, sin = jnp.cos(ang), jnp.sin(ang)
    cosf = jnp.repeat(cos, 2, axis=-1)
    sinf = jnp.stack([-sin, sin], axis=-1).reshape(seq_len, HEAD_DIM)
    return cosf, sinf


def _rope_kernel(q_ref, k_ref, cos_ref, sin_ref, qg_ref, kg_ref, qo_ref, ko_ref, *, q_scale):
    cos = cos_ref[...]
    sin = sin_ref[...]
    lane = lax.broadcasted_iota(jnp.int32, cos.shape, 1)
    even = (lane & 1) == 0

    def norm_rope(x, g):
        x = x.astype(F32)
        ms = jnp.mean(x * x, axis=-1, keepdims=True)
        xn = x * lax.rsqrt(ms + NORM_EPS) * g
        nxt = pltpu.roll(xn, HEAD_DIM - 1, 1)
        prv = pltpu.roll(xn, 1, 1)
        return xn * cos + jnp.where(even, nxt, prv) * sin

    qg = qg_ref[...]
    kg = kg_ref[...]
    for h in range(N_Q_HEADS):
        sl = slice(h * HEAD_DIM, (h + 1) * HEAD_DIM)
        qo_ref[:, sl] = (norm_rope(q_ref[:, sl], qg) * q_scale).astype(qo_ref.dtype)
    for h in range(N_KV_HEADS):
        sl = slice(h * HEAD_DIM, (h + 1) * HEAD_DIM)
        ko_ref[:, sl] = norm_rope(k_ref[:, sl], kg).astype(ko_ref.dtype)


def _rope(z3, q_norm, k_norm):
    b, s, _ = z3.shape
    tr = _tile(s, 256)
    cosf, sinf = _rope_tables(s)
    k_blk = ATTN_WIDTH // KV_WIDTH
    return pl.pallas_call(
        functools.partial(_rope_kernel, q_scale=HEAD_DIM ** -0.5),
        grid=(b, s // tr),
        in_specs=[pl.BlockSpec((None, tr, ATTN_WIDTH), lambda bi, i: (bi, i, 0)),
                  pl.BlockSpec((None, tr, KV_WIDTH), lambda bi, i: (bi, i, k_blk)),
                  pl.BlockSpec((tr, HEAD_DIM), lambda bi, i: (i, 0)),
                  pl.BlockSpec((tr, HEAD_DIM), lambda bi, i: (i, 0)),
                  pl.BlockSpec((1, HEAD_DIM), lambda bi, i: (0, 0)),
                  pl.BlockSpec((1, HEAD_DIM), lambda bi, i: (0, 0))],
        out_specs=[pl.BlockSpec((None, tr, ATTN_WIDTH), lambda bi, i: (bi, i, 0)),
                   pl.BlockSpec((None, tr, KV_WIDTH), lambda bi, i: (bi, i, 0))],
        out_shape=[jax.ShapeDtypeStruct((b, s, ATTN_WIDTH), BF16),
                   jax.ShapeDtypeStruct((b, s, KV_WIDTH), BF16)],
        compiler_params=_params(2),
        name="qk_norm_rope",
    )(z3, z3, cosf, sinf, q_norm.reshape(1, HEAD_DIM), k_norm.reshape(1, HEAD_DIM))


def _attn_kernel(q_ref, k_ref, v_ref, o_ref, m_scr, l_scr, acc_scr, *, tk):
    tq = q_ref.shape[0]
    n_chunks = k_ref.shape[0] // tk
    q = jnp.concatenate(
        [q_ref[:, g * HEAD_DIM:(g + 1) * HEAD_DIM] for g in range(GQA_GROUP)], axis=0)
    m_scr[...] = jnp.full(m_scr.shape, -jnp.inf, F32)
    l_scr[...] = jnp.zeros(l_scr.shape, F32)
    acc_scr[...] = jnp.zeros(acc_scr.shape, F32)

    def body(c, carry):
        start = pl.multiple_of(c * tk, tk)
        kc = k_ref[pl.ds(start, tk), :]
        vc = v_ref[pl.ds(start, tk), :]
        s = lax.dot_general(q, kc, (((1,), (1,)), ((), ())), preferred_element_type=F32)
        m_old = m_scr[...]
        m_new = jnp.maximum(m_old, jnp.max(s, axis=1, keepdims=True))
        alpha = jnp.exp(m_old - m_new)
        p = jnp.exp(s - m_new)
        l_scr[...] = alpha * l_scr[...] + jnp.sum(p, axis=1, keepdims=True)
        acc_scr[...] = alpha * acc_scr[...] + jnp.dot(
            p.astype(vc.dtype), vc, preferred_element_type=F32)
        m_scr[...] = m_new
        return carry

    lax.fori_loop(0, n_chunks, body, 0)
    out = acc_scr[...] / l_scr[...]
    for g in range(GQA_GROUP):
        o_ref[:, g * HEAD_DIM:(g + 1) * HEAD_DIM] = out[g * tq:(g + 1) * tq].astype(o_ref.dtype)


def _attention(qr, kr, z3, v_col0):
    b, s, _ = qr.shape
    tq = _tile(s, 128)
    tk = _tile(s, 512)
    rows = GQA_GROUP * tq
    gw = GQA_GROUP * HEAD_DIM
    v_blk = v_col0 // HEAD_DIM
    return pl.pallas_call(
        functools.partial(_attn_kernel, tk=tk),
        grid=(b, N_KV_HEADS, s // tq),
        in_specs=[pl.BlockSpec((None, tq, gw), lambda bi, h, i: (bi, i, h)),
                  pl.BlockSpec((None, s, HEAD_DIM), lambda bi, h, i: (bi, 0, h)),
                  pl.BlockSpec((None, s, HEAD_DIM), lambda bi, h, i: (bi, 0, v_blk + h))],
        out_specs=pl.BlockSpec((None, tq, gw), lambda bi, h, i: (bi, i, h)),
        out_shape=jax.ShapeDtypeStruct((b, s, ATTN_WIDTH), BF16),
        scratch_shapes=[pltpu.VMEM((rows, 1), F32),
                        pltpu.VMEM((rows, 1), F32),
                        pltpu.VMEM((rows, HEAD_DIM), F32)],
        compiler_params=_params(3),
        name="gqa_attention",
    )(qr, kr, z3)


def _softplus(x):
    return jnp.maximum(x, 0.0) + jnp.log1p(jnp.exp(-jnp.abs(x)))


def _gelu_tanh(x):
    c = math.sqrt(2.0 / math.pi)
    return 0.5 * x * (1.0 + jnp.tanh(c * (x + 0.044715 * (x * x * x))))


def _lru_kernel(xr_ref, yr_ref, cw_ref, cb_ref, wg_ref, bg_ref, lam_ref, o_ref,
                xpad, hf, hb, *, chunk):
    s = xr_ref.shape[0]
    n_chunks = s // chunk
    n_tiles = chunk // 8
    w = LANES

    xpad[0:HALO, :] = jnp.zeros((HALO, w), F32)
    xpad[s + HALO:s + 2 * HALO, :] = jnp.zeros((HALO, w), F32)

    def fill(c, carry):
        t0 = pl.multiple_of(c * chunk, chunk)
        xpad[pl.ds(t0 + HALO, chunk), :] = xr_ref[pl.ds(t0, chunk), :].astype(F32)
        return carry

    lax.fori_loop(0, n_chunks, fill, 0)

    cw = cw_ref[...]
    cb = cb_ref[...]
    sp = _softplus(-lam_ref[...])
    sub = lax.broadcasted_iota(jnp.int32, (1, 8, w), 1)
    ext = chunk + 2 * HALO

    def conv(t0):
        e = xpad[pl.ds(t0, ext), :]
        out = cb + cw[0:1] * pltpu.roll(e, 1, 0)[HALO:HALO + chunk]
        out = out + cw[1:2] * e[HALO:HALO + chunk]
        out = out + cw[2:3] * pltpu.roll(e, ext - 1, 0)[HALO:HALO + chunk]
        out = out + cw[3:4] * pltpu.roll(e, ext - 2, 0)[HALO:HALO + chunk]
        return out

    def gate_terms(xc, direction):
        col = 2 * w * direction
        g = jnp.dot(xc.astype(BF16), wg_ref[:, col:col + 2 * w],
                    preferred_element_type=F32) + bg_ref[:, col:col + 2 * w]
        r = jax.nn.sigmoid(g[:, :w])
        i = jax.nn.sigmoid(g[:, w:])
        log_a = (-LRU_C * r) * sp[direction:direction + 1]
        a = jnp.exp(log_a)
        u = jnp.sqrt(-jnp.tanh(log_a) * (a * a + 1.0)) * (i * xc)
        return a.reshape(n_tiles, 8, w), u.reshape(n_tiles, 8, w)

    def tile_scan(a, u, reverse):
        for d in (1, 2, 4):
            shift = (8 - d) if reverse else d
            keep = (sub < 8 - d) if reverse else (sub >= d)
            a_sh = pltpu.roll(a, shift, 1)
            u_sh = pltpu.roll(u, shift, 1)
            u = jnp.where(keep, u + a * u_sh, u)
            a = jnp.where(keep, a * a_sh, a)
        return a, u

    def body(c, carry):
        h_f, h_b = carry
        tf = pl.multiple_of(c * chunk, chunk)
        tb = pl.multiple_of((n_chunks - 1 - c) * chunk, chunk)
        a_f, u_f = tile_scan(*gate_terms(conv(tf), 0), reverse=False)
        a_b, u_b = tile_scan(*gate_terms(conv(tb), 1), reverse=True)
        for k in range(n_tiles):
            ht = u_f[k] + a_f[k] * h_f
            hf[pl.ds(tf + 8 * k, 8), :] = ht
            h_f = jnp.broadcast_to(ht[7:8, :], (8, w))
            kb = n_tiles - 1 - k
            ht = u_b[kb] + a_b[kb] * h_b
            hb[pl.ds(tb + 8 * kb, 8), :] = ht
            h_b = jnp.broadcast_to(ht[0:1, :], (8, w))
        return h_f, h_b

    zero = jnp.zeros((8, w), F32)
    lax.fori_loop(0, n_chunks, body, (zero, zero))

    def emit(c, carry):
        t0 = pl.multiple_of(c * chunk, chunk)
        y = yr_ref[pl.ds(t0, chunk), :].astype(F32)
        rec = (hf[pl.ds(t0, chunk), :] + hb[pl.ds(t0, chunk), :]) * _gelu_tanh(y)
        o_ref[pl.ds(t0, chunk), :] = rec.astype(o_ref.dtype)
        return carry

    lax.fori_loop(0, n_chunks, emit, 0)


def _lru(z3, conv_w, conv_b, w_a, b_a, w_i, b_i, lam, x_col0, y_col0):
    b, s, _ = z3.shape
    n_blocks = w_a.shape[1]
    d_rnn = n_blocks * LANES
    chunk = _tile(s, 256)
    wg = jnp.concatenate([w_a[0], w_i[0], w_a[1], w_i[1]], axis=-1).astype(BF16)
    bg = jnp.concatenate(
        [v.reshape(n_blocks, 1, LANES) for v in (b_a[0], b_i[0], b_a[1], b_i[1])], axis=-1)
    x_blk = x_col0 // LANES
    y_blk = y_col0 // LANES
    return pl.pallas_call(
        functools.partial(_lru_kernel, chunk=chunk),
        grid=(b, n_blocks),
        in_specs=[pl.BlockSpec((None, s, LANES), lambda bi, j: (bi, 0, x_blk + j)),
                  pl.BlockSpec((None, s, LANES), lambda bi, j: (bi, 0, y_blk + j)),
                  pl.BlockSpec((CONV_W, LANES), lambda bi, j: (0, j)),
                  pl.BlockSpec((1, LANES), lambda bi, j: (0, j)),
                  pl.BlockSpec((None, LANES, 4 * LANES), lambda bi, j: (j, 0, 0)),
                  pl.BlockSpec((None, 1, 4 * LANES), lambda bi, j: (j, 0, 0)),
                  pl.BlockSpec((2, LANES), lambda bi, j: (0, j))],
        out_specs=pl.BlockSpec((None, s, LANES), lambda bi, j: (bi, 0, j)),
        out_shape=jax.ShapeDtypeStruct((b, s, d_rnn), BF16),
        scratch_shapes=[pltpu.VMEM((s + 2 * HALO, LANES), F32),
                        pltpu.VMEM((s, LANES), F32),
                        pltpu.VMEM((s, LANES), F32)],
        compiler_params=_params(2),
        name="conv_rglru",
    )(z3, z3, conv_w, conv_b.reshape(1, d_rnn), wg, bg, lam)


def _encoder_layer(x2, b, s, p):
    d = x2.shape[1]
    d_rnn = p["w_rnn_out"].shape[0]
    c_k = ATTN_WIDTH
    c_v = c_k + KV_WIDTH
    c_x = c_v + KV_WIDTH
    c_y = c_x + d_rnn
    c_g = c_y + d_rnn

    h = _rmsnorm_cast(x2, p["norm_mix"])
    z = _matmul(h, p["w_in"], "in_proj")
    z3 = z.reshape(b, s, z.shape[1])

    qr, kr = _rope(z3, p["q_norm"], p["k_norm"])
    attn = _attention(qr, kr, z3, c_v)
    rec = _lru(z3, p["conv_w"], p["conv_b"], p["lru_w_a"], p["lru_b_a"], p["lru_w_i"],
               p["lru_b_i"], p["lru_lambda"], c_x, c_y)

    merged = _merge(attn.reshape(b * s, ATTN_WIDTH), rec.reshape(b * s, d_rnn),
                    p["w_attn_out"], p["w_rnn_out"], z, p["b_gate"], c_g)
    x1 = _outproj(merged, p["w_out"], x2)
    hm = _rmsnorm_cast(x1, p["norm_mlp"])
    up = _mlp_up(hm, p["w_up"])
    return up, x1


def _trunk(x, layers, norm_final):
    b, s, d = x.shape
    x2 = x.reshape(b * s, d)
    for li, p in enumerate(layers):
        up, x1 = _encoder_layer(x2, b, s, p)
        x2 = _mlp_down(up, p["w_down"], x1, norm_final, final_norm=(li == len(layers) - 1))
    return x2.reshape(b, s, d)


def kernel(x_prompt, x_sample, norm_mix, w_in, q_norm, k_norm, conv_w, conv_b, lru_w_a, lru_b_a,
           lru_w_i, lru_b_i, lru_lambda, w_attn_out, w_rnn_out, b_gate, w_out, norm_mlp, w_up,
           w_down, norm_final):
    depth = w_in.shape[0]
    layers = []
    for l in range(depth):
        layers.append(dict(
            norm_mix=norm_mix[l], w_in=w_in[l].astype(BF16), q_norm=q_norm[l], k_norm=k_norm[l],
            conv_w=conv_w[l], conv_b=conv_b[l], lru_w_a=lru_w_a[l], lru_b_a=lru_b_a[l],
            lru_w_i=lru_w_i[l], lru_b_i=lru_b_i[l], lru_lambda=lru_lambda[l],
            w_attn_out=w_attn_out[l].astype(BF16), w_rnn_out=w_rnn_out[l].astype(BF16),
            b_gate=b_gate[l], w_out=w_out[l].astype(BF16), norm_mlp=norm_mlp[l],
            w_up=w_up[l].astype(BF16), w_down=w_down[l].astype(BF16)))
    y_prompt = _trunk(x_prompt, layers, norm_final)
    y_sample = _trunk(x_sample, layers, norm_final)
    return (y_prompt, y_sample)
```

```python
import functools
import math

import jax
import jax.numpy as jnp
from jax import lax
from jax.experimental import pallas as pl
from jax.experimental.pallas import tpu as pltpu

F32 = jnp.float32
BF16 = jnp.bfloat16

HEAD_DIM = 128
N_Q_HEADS = 16
N_KV_HEADS = 4
GQA_GROUP = N_Q_HEADS // N_KV_HEADS
ATTN_WIDTH = N_Q_HEADS * HEAD_DIM
KV_WIDTH = N_KV_HEADS * HEAD_DIM
ROPE_THETA = 10000.0
GRID_W = 64
LRU_C = 8.0
CONV_W = 4
NORM_EPS = 1e-6
LANES = 128
HALO = 16
V7X_VMEM_LIMIT = 56 * 1024 * 1024


def _params(n_axes, vmem=V7X_VMEM_LIMIT):
    return pltpu.CompilerParams(
        dimension_semantics=("arbitrary",) * n_axes, vmem_limit_bytes=vmem)


def _tile(n, want):
    t = min(n, want)
    while n % t:
        t //= 2
    return t


def _rmsnorm_cast_kernel(x_ref, g_ref, o_ref):
    x = x_ref[...]
    ms = jnp.mean(x * x, axis=-1, keepdims=True)
    o_ref[...] = (x * lax.rsqrt(ms + NORM_EPS) * g_ref[...]).astype(o_ref.dtype)


def _rmsnorm_cast(x, g):
    t, d = x.shape
    tr = _tile(t, 256)
    return pl.pallas_call(
        _rmsnorm_cast_kernel,
        grid=(t // tr,),
        in_specs=[pl.BlockSpec((tr, d), lambda i: (i, 0)),
                  pl.BlockSpec((1, d), lambda i: (0, 0))],
        out_specs=pl.BlockSpec((tr, d), lambda i: (i, 0)),
        out_shape=jax.ShapeDtypeStruct((t, d), BF16),
        compiler_params=_params(1),
        name="rmsnorm_cast",
    )(x, g.reshape(1, d))


def _mm_kernel(a_ref, w_ref, o_ref):
    o_ref[...] = jnp.dot(a_ref[...], w_ref[...],
                         preferred_element_type=F32).astype(o_ref.dtype)


def _matmul(a, w, name):
    m, k = a.shape
    n = w.shape[1]
    tm, tn = _tile(m, 1024), _tile(n, 1024)
    return pl.pallas_call(
        _mm_kernel,
        grid=(m // tm, n // tn),
        in_specs=[pl.BlockSpec((tm, k), lambda i, j: (i, 0)),
                  pl.BlockSpec((k, tn), lambda i, j: (0, j))],
        out_specs=pl.BlockSpec((tm, tn), lambda i, j: (i, j)),
        out_shape=jax.ShapeDtypeStruct((m, n), BF16),
        compiler_params=_params(2),
        name=name,
    )(a, w)


def _mlp_up_kernel(a_ref, w_ref, o_ref):
    y = jnp.dot(a_ref[...], w_ref[...], preferred_element_type=F32)
    r = jnp.maximum(y, 0.0)
    o_ref[...] = (r * r).astype(o_ref.dtype)


def _mlp_up(a, w):
    m, k = a.shape
    n = w.shape[1]
    tm, tn = _tile(m, 1024), _tile(n, 1024)
    return pl.pallas_call(
        _mlp_up_kernel,
        grid=(m // tm, n // tn),
        in_specs=[pl.BlockSpec((tm, k), lambda i, j: (i, 0)),
                  pl.BlockSpec((k, tn), lambda i, j: (0, j))],
        out_specs=pl.BlockSpec((tm, tn), lambda i, j: (i, j)),
        out_shape=jax.ShapeDtypeStruct((m, n), BF16),
        compiler_params=_params(2),
        name="mlp_up",
    )(a, w)


def _merge_kernel(attn_ref, rec_ref, wa_ref, wr_ref, ga_ref, gr_ref, ba_ref, br_ref, o_ref):
    ab = jnp.dot(attn_ref[...], wa_ref[...], preferred_element_type=F32)
    rb = jnp.dot(rec_ref[...], wr_ref[...], preferred_element_type=F32)
    ga = jax.nn.sigmoid(ga_ref[...].astype(F32) + ba_ref[...])
    gr = jax.nn.sigmoid(gr_ref[...].astype(F32) + br_ref[...])
    o_ref[...] = (ga * ab + gr * rb).astype(o_ref.dtype)


def _merge(attn, rec, wa, wr, z, b_gate, gate_col0):
    m, ka = attn.shape
    kr = rec.shape[1]
    n = wa.shape[1]
    tm, tn = _tile(m, 1024), _tile(n, 512)
    ga_blk = gate_col0 // tn
    gr_blk = (gate_col0 + n) // tn
    return pl.pallas_call(
        _merge_kernel,
        grid=(m // tm, n // tn),
        in_specs=[pl.BlockSpec((tm, ka), lambda i, j: (i, 0)),
                  pl.BlockSpec((tm, kr), lambda i, j: (i, 0)),
                  pl.BlockSpec((ka, tn), lambda i, j: (0, j)),
                  pl.BlockSpec((kr, tn), lambda i, j: (0, j)),
                  pl.BlockSpec((tm, tn), lambda i, j: (i, ga_blk + j)),
                  pl.BlockSpec((tm, tn), lambda i, j: (i, gr_blk + j)),
                  pl.BlockSpec((1, tn), lambda i, j: (0, j)),
                  pl.BlockSpec((1, tn), lambda i, j: (0, j))],
        out_specs=pl.BlockSpec((tm, tn), lambda i, j: (i, j)),
        out_shape=jax.ShapeDtypeStruct((m, n), BF16),
        compiler_params=_params(2),
        name="branch_merge",
    )(attn, rec, wa, wr, z, z, b_gate[0:1], b_gate[1:2])


def _outproj_kernel(a_ref, w_ref, x_ref, o_ref):
    o_ref[...] = x_ref[...] + jnp.dot(a_ref[...], w_ref[...], preferred_element_type=F32)


def _outproj(a, w, x):
    m, k = a.shape
    n = w.shape[1]
    tm, tn = _tile(m, 1024), _tile(n, 512)
    return pl.pallas_call(
        _outproj_kernel,
        grid=(m // tm, n // tn),
        in_specs=[pl.BlockSpec((tm, k), lambda i, j: (i, 0)),
                  pl.BlockSpec((k, tn), lambda i, j: (0, j)),
                  pl.BlockSpec((tm, tn), lambda i, j: (i, j))],
        out_specs=pl.BlockSpec((tm, tn), lambda i, j: (i, j)),
        out_shape=jax.ShapeDtypeStruct((m, n), F32),
        compiler_params=_params(2),
        name="out_proj",
    )(a, w, x)


def _mlp_down_kernel(u_ref, w_ref, x_ref, g_ref, o_ref, *, final_norm):
    kk = pl.program_id(1)

    @pl.when(kk == 0)
    def _():
        o_ref[...] = x_ref[...]

    o_ref[...] += jnp.dot(u_ref[...], w_ref[...], preferred_element_type=F32)

    if final_norm:
        @pl.when(kk == pl.num_programs(1) - 1)
        def _():
            x = o_ref[...]
            ms = jnp.mean(x * x, axis=-1, keepdims=True)
            o_ref[...] = x * lax.rsqrt(ms + NORM_EPS) * g_ref[...]


def _mlp_down(u, w, x, g, final_norm):
    m, k = u.shape
    n = w.shape[1]
    tm, tk = _tile(m, 512), _tile(k, 512)
    return pl.pallas_call(
        functools.partial(_mlp_down_kernel, final_norm=final_norm),
        grid=(m // tm, k // tk),
        in_specs=[pl.BlockSpec((tm, tk), lambda i, kk: (i, kk)),
                  pl.BlockSpec((tk, n), lambda i, kk: (kk, 0)),
                  pl.BlockSpec((tm, n), lambda i, kk: (i, 0)),
                  pl.BlockSpec((1, n), lambda i, kk: (0, 0))],
        out_specs=pl.BlockSpec((tm, n), lambda i, kk: (i, 0)),
        out_shape=jax.ShapeDtypeStruct((m, n), F32),
        compiler_params=_params(2),
        name="mlp_down_norm",
    )(u, w, x, g.reshape(1, n))


Q_SCALE = math.log2(math.e) * HEAD_DIM ** -0.5


def _rope_tables(seq_len):
    axis_dim = HEAD_DIM // 2
    n_rows = seq_len // GRID_W
    row = jnp.repeat(jnp.arange(n_rows), GRID_W).astype(F32)
    col = jnp.tile(jnp.arange(GRID_W), n_rows).astype(F32)
    inv = ROPE_THETA ** (-jnp.arange(0, axis_dim, 2, dtype=F32) / axis_dim)
    ang = jnp.concatenate([row[:, None] * inv, col[:, None] * inv], axis=-1)
    cos, sin = jnp.cos(ang), jnp.sin(ang)
    cosf = jnp.repeat(cos, 2, axis=-1)
    sinf = jnp.stack([-sin, sin], axis=-1).reshape(seq_len, HEAD_DIM)
    return cosf, sinf


def _rope_kernel(q_ref, k_ref, v_ref, cos_ref, sin_ref, qg_ref, kg_ref, qt_ref, ko_ref, vt_ref):
    cos = cos_ref[...]
    sin = sin_ref[...]
    lane = lax.broadcasted_iota(jnp.int32, cos.shape, 1)
    even = (lane & 1) == 0

    def norm_rope(x, g):
        x = x.astype(F32)
        ms = jnp.mean(x * x, axis=-1, keepdims=True)
        xn = x * lax.rsqrt(ms + NORM_EPS) * g
        nxt = pltpu.roll(xn, HEAD_DIM - 1, 1)
        prv = pltpu.roll(xn, 1, 1)
        return xn * cos + jnp.where(even, nxt, prv) * sin

    qg = qg_ref[...]
    kg = kg_ref[...]
    for h in range(N_Q_HEADS):
        sl = slice(h * HEAD_DIM, (h + 1) * HEAD_DIM)
        qt_ref[sl, :] = (norm_rope(q_ref[:, sl], qg) * Q_SCALE).T.astype(qt_ref.dtype)
    for h in range(N_KV_HEADS):
        sl = slice(h * HEAD_DIM, (h + 1) * HEAD_DIM)
        ko_ref[:, sl] = norm_rope(k_ref[:, sl], kg).astype(ko_ref.dtype)
        vt_ref[sl, :] = v_ref[:, sl].astype(F32).T.astype(vt_ref.dtype)


def _rope(z3, q_norm, k_norm):
    b, s, _ = z3.shape
    tr = _tile(s, 256)
    cosf, sinf = _rope_tables(s)
    k_blk = ATTN_WIDTH // KV_WIDTH
    return pl.pallas_call(
        _rope_kernel,
        grid=(b, s // tr),
        in_specs=[pl.BlockSpec((None, tr, ATTN_WIDTH), lambda bi, i: (bi, i, 0)),
                  pl.BlockSpec((None, tr, KV_WIDTH), lambda bi, i: (bi, i, k_blk)),
                  pl.BlockSpec((None, tr, KV_WIDTH), lambda bi, i: (bi, i, k_blk + 1)),
                  pl.BlockSpec((tr, HEAD_DIM), lambda bi, i: (i, 0)),
                  pl.BlockSpec((tr, HEAD_DIM), lambda bi, i: (i, 0)),
                  pl.BlockSpec((1, HEAD_DIM), lambda bi, i: (0, 0)),
                  pl.BlockSpec((1, HEAD_DIM), lambda bi, i: (0, 0))],
        out_specs=[pl.BlockSpec((None, ATTN_WIDTH, tr), lambda bi, i: (bi, 0, i)),
                   pl.BlockSpec((None, tr, KV_WIDTH), lambda bi, i: (bi, i, 0)),
                   pl.BlockSpec((None, KV_WIDTH, tr), lambda bi, i: (bi, 0, i))],
        out_shape=[jax.ShapeDtypeStruct((b, ATTN_WIDTH, s), BF16),
                   jax.ShapeDtypeStruct((b, s, KV_WIDTH), BF16),
                   jax.ShapeDtypeStruct((b, KV_WIDTH, s), BF16)],
        compiler_params=_params(2),
        name="qk_norm_rope",
    )(z3, z3, z3, cosf, sinf, q_norm.reshape(1, HEAD_DIM), k_norm.reshape(1, HEAD_DIM))


def _attn_kernel(qt_ref, k_ref, vt_ref, o_ref, qt_scr, st_scr, acc_scr, *, tk):
    tq = qt_ref.shape[1]
    n_pairs = k_ref.shape[0] // (2 * tk)
    for g in range(GQA_GROUP):
        qt_scr[:, g * tq:(g + 1) * tq] = qt_ref[g * HEAD_DIM:(g + 1) * HEAD_DIM, :]
    acc_scr[...] = jnp.zeros(acc_scr.shape, F32)

    def scores(c, slot):
        start = pl.multiple_of(c * tk, tk)
        st_scr[slot] = jnp.dot(k_ref[pl.ds(start, tk), :], qt_scr[...],
                               preferred_element_type=F32)

    def softmax_pv(c, slot, carry):
        m_old, l_old = carry
        start = pl.multiple_of(c * tk, tk)
        vc = vt_ref[:, pl.ds(start, tk)]
        st = st_scr[slot]
        m_new = jnp.maximum(m_old, jnp.max(st, axis=0, keepdims=True))
        alpha = jnp.exp2(m_old - m_new)
        p = jnp.exp2(st - m_new)
        l_new = alpha * l_old + jnp.sum(p, axis=0, keepdims=True)
        acc_scr[...] = alpha * acc_scr[...] + jnp.dot(
            vc, p.astype(vc.dtype), preferred_element_type=F32)
        return m_new, l_new

    def pair(j, carry, last):
        c = 2 * j
        scores(c + 1, 1)
        carry = softmax_pv(c, 0, carry)
        if not last:
            scores(c + 2, 0)
        return softmax_pv(c + 1, 1, carry)

    rows = acc_scr.shape[1]
    carry = (jnp.full((1, rows), -jnp.inf, F32), jnp.zeros((1, rows), F32))
    scores(0, 0)
    carry = lax.fori_loop(0, n_pairs - 1, lambda j, cr: pair(j, cr, False), carry)
    _, l_fin = pair(n_pairs - 1, carry, True)
    out = acc_scr[...] * (1.0 / l_fin)
    for g in range(GQA_GROUP):
        o_ref[:, g * HEAD_DIM:(g + 1) * HEAD_DIM] = out[:, g * tq:(g + 1) * tq].T.astype(o_ref.dtype)


def _attention(qt, kr, vt):
    b, s, _ = kr.shape
    tq = _tile(s, 128)
    tk = _tile(s // 2, 512)
    rows = GQA_GROUP * tq
    gw = GQA_GROUP * HEAD_DIM
    return pl.pallas_call(
        functools.partial(_attn_kernel, tk=tk),
        grid=(b, N_KV_HEADS, s // tq),
        in_specs=[pl.BlockSpec((None, gw, tq), lambda bi, h, i: (bi, h, i)),
                  pl.BlockSpec((None, s, HEAD_DIM), lambda bi, h, i: (bi, 0, h)),
                  pl.BlockSpec((None, HEAD_DIM, s), lambda bi, h, i: (bi, h, 0))],
        out_specs=pl.BlockSpec((None, tq, gw), lambda bi, h, i: (bi, i, h)),
        out_shape=jax.ShapeDtypeStruct((b, s, ATTN_WIDTH), BF16),
        scratch_shapes=[pltpu.VMEM((HEAD_DIM, rows), BF16),
                        pltpu.VMEM((2, tk, rows), F32),
                        pltpu.VMEM((HEAD_DIM, rows), F32)],
        compiler_params=_params(3),
        name="gqa_attention",
    )(qt, kr, vt)


def _softplus(x):
    return jnp.maximum(x, 0.0) + jnp.log1p(jnp.exp(-jnp.abs(x)))


def _gelu_tanh(x):
    c = math.sqrt(2.0 / math.pi)
    return 0.5 * x * (1.0 + jnp.tanh(c * (x + 0.044715 * (x * x * x))))


def _lru_kernel(xr_ref, yr_ref, cw_ref, cb_ref, wg_ref, bg_ref, lam_ref, o_ref,
                xpad, hf, hb, *, chunk):
    s = xr_ref.shape[0]
    n_chunks = s // chunk
    n_tiles = chunk // 8
    w = LANES

    xpad[0:HALO, :] = jnp.zeros((HALO, w), F32)
    xpad[s + HALO:s + 2 * HALO, :] = jnp.zeros((HALO, w), F32)

    def fill(c, carry):
        t0 = pl.multiple_of(c * chunk, chunk)
        xpad[pl.ds(t0 + HALO, chunk), :] = xr_ref[pl.ds(t0, chunk), :].astype(F32)
        return carry

    lax.fori_loop(0, n_chunks, fill, 0)

    cw = cw_ref[...]
    cb = cb_ref[...]
    sp = _softplus(-lam_ref[...])
    sub = lax.broadcasted_iota(jnp.int32, (1, 8, w), 1)
    ext = chunk + 2 * HALO

    def conv(t0):
        e = xpad[pl.ds(t0, ext), :]
        out = cb + cw[0:1] * pltpu.roll(e, 1, 0)[HALO:HALO + chunk]
        out = out + cw[1:2] * e[HALO:HALO + chunk]
        out = out + cw[2:3] * pltpu.roll(e, ext - 1, 0)[HALO:HALO + chunk]
        out = out + cw[3:4] * pltpu.roll(e, ext - 2, 0)[HALO:HALO + chunk]
        return out

    def gate_terms(xc, direction):
        col = 2 * w * direction
        g = jnp.dot(xc.astype(BF16), wg_ref[:, col:col + 2 * w],
                    preferred_element_type=F32) + bg_ref[:, col:col + 2 * w]
        r = jax.nn.sigmoid(g[:, :w])
        i = jax.nn.sigmoid(g[:, w:])
        log_a = (-LRU_C * r) * sp[direction:direction + 1]
        a = jnp.exp(log_a)
        u = jnp.sqrt(-jnp.tanh(log_a) * (a * a + 1.0)) * (i * xc)
        return a.reshape(n_tiles, 8, w), u.reshape(n_tiles, 8, w)

    def tile_scan(a, u, reverse):
        for d in (1, 2, 4):
            shift = (8 - d) if reverse else d
            keep = (sub < 8 - d) if reverse else (sub >= d)
            a_sh = pltpu.roll(a, shift, 1)
            u_sh = pltpu.roll(u, shift, 1)
            u = jnp.where(keep, u + a * u_sh, u)
            a = jnp.where(keep, a * a_sh, a)
        return a, u

    def body(c, carry):
        h_f, h_b = carry
        tf = pl.multiple_of(c * chunk, chunk)
        tb = pl.multiple_of((n_chunks - 1 - c) * chunk, chunk)
        a_f, u_f = tile_scan(*gate_terms(conv(tf), 0), reverse=False)
        a_b, u_b = tile_scan(*gate_terms(conv(tb), 1), reverse=True)
        for k in range(n_tiles):
            ht = u_f[k] + a_f[k] * h_f
            hf[pl.ds(tf + 8 * k, 8), :] = ht
            h_f = jnp.broadcast_to(ht[7:8, :], (8, w))
            kb = n_tiles - 1 - k
            ht = u_b[kb] + a_b[kb] * h_b
            hb[pl.ds(tb + 8 * kb, 8), :] = ht
            h_b = jnp.broadcast_to(ht[0:1, :], (8, w))
        return h_f, h_b

    zero = jnp.zeros((8, w), F32)
    lax.fori_loop(0, n_chunks, body, (zero, zero))

    def emit(c, carry):
        t0 = pl.multiple_of(c * chunk, chunk)
        y = yr_ref[pl.ds(t0, chunk), :].astype(F32)
        rec = (hf[pl.ds(t0, chunk), :] + hb[pl.ds(t0, chunk), :]) * _gelu_tanh(y)
        o_ref[pl.ds(t0, chunk), :] = rec.astype(o_ref.dtype)
        return carry

    lax.fori_loop(0, n_chunks, emit, 0)


def _lru(z3, conv_w, conv_b, w_a, b_a, w_i, b_i, lam, x_col0, y_col0):
    b, s, _ = z3.shape
    n_blocks = w_a.shape[1]
    d_rnn = n_blocks * LANES
    chunk = _tile(s, 256)
    wg = jnp.concatenate([w_a[0], w_i[0], w_a[1], w_i[1]], axis=-1).astype(BF16)
    bg = jnp.concatenate(
        [v.reshape(n_blocks, 1, LANES) for v in (b_a[0], b_i[0], b_a[1], b_i[1])], axis=-1)
    x_blk = x_col0 // LANES
    y_blk = y_col0 // LANES
    return pl.pallas_call(
        functools.partial(_lru_kernel, chunk=chunk),
        grid=(b, n_blocks),
        in_specs=[pl.BlockSpec((None, s, LANES), lambda bi, j: (bi, 0, x_blk + j)),
                  pl.BlockSpec((None, s, LANES), lambda bi, j: (bi, 0, y_blk + j)),
                  pl.BlockSpec((CONV_W, LANES), lambda bi, j: (0, j)),
                  pl.BlockSpec((1, LANES), lambda bi, j: (0, j)),
                  pl.BlockSpec((None, LANES, 4 * LANES), lambda bi, j: (j, 0, 0)),
                  pl.BlockSpec((None, 1, 4 * LANES), lambda bi, j: (j, 0, 0)),
                  pl.BlockSpec((2, LANES), lambda bi, j: (0, j))],
        out_specs=pl.BlockSpec((None, s, LANES), lambda bi, j: (bi, 0, j)),
        out_shape=jax.ShapeDtypeStruct((b, s, d_rnn), BF16),
        scratch_shapes=[pltpu.VMEM((s + 2 * HALO, LANES), F32),
                        pltpu.VMEM((s, LANES), F32),
                        pltpu.VMEM((s, LANES), F32)],
        compiler_params=_params(2),
        name="conv_rglru",
    )(z3, z3, conv_w, conv_b.reshape(1, d_rnn), wg, bg, lam)


def _encoder_layer(x2, b, s, p):
    d = x2.shape[1]
    d_rnn = p["w_rnn_out"].shape[0]
    c_k = ATTN_WIDTH
    c_v = c_k + KV_WIDTH
    c_x = c_v + KV_WIDTH
    c_y = c_x + d_rnn
    c_g = c_y + d_rnn

    h = _rmsnorm_cast(x2, p["norm_mix"])
    z = _matmul(h, p["w_in"], "in_proj")
    z3 = z.reshape(b, s, z.shape[1])

    qt, kr, vt = _rope(z3, p["q_norm"], p["k_norm"])
    attn = _attention(qt, kr, vt)
    rec = _lru(z3, p["conv_w"], p["conv_b"], p["lru_w_a"], p["lru_b_a"], p["lru_w_i"],
               p["lru_b_i"], p["lru_lambda"], c_x, c_y)

    merged = _merge(attn.reshape(b * s, ATTN_WIDTH), rec.reshape(b * s, d_rnn),
                    p["w_attn_out"], p["w_rnn_out"], z, p["b_gate"], c_g)
    x1 = _outproj(merged, p["w_out"], x2)
    hm = _rmsnorm_cast(x1, p["norm_mlp"])
    up = _mlp_up(hm, p["w_up"])
    return up, x1


def _trunk(x, layers, norm_final):
    b, s, d = x.shape
    x2 = x.reshape(b * s, d)
    for li, p in enumerate(layers):
        up, x1 = _encoder_layer(x2, b, s, p)
        x2 = _mlp_down(up, p["w_down"], x1, norm_final, final_norm=(li == len(layers) - 1))
    return x2.reshape(b, s, d)


def kernel(x_prompt, x_sample, norm_mix, w_in, q_norm, k_norm, conv_w, conv_b, lru_w_a, lru_b_a,
           lru_w_i, lru_b_i, lru_lambda, w_attn_out, w_rnn_out, b_gate, w_out, norm_mlp, w_up,
           w_down, norm_final):
    depth = w_in.shape[0]
    layers = []
    for l in range(depth):
        layers.append(dict(
            norm_mix=norm_mix[l], w_in=w_in[l].astype(BF16), q_norm=q_norm[l], k_norm=k_norm[l],
            conv_w=conv_w[l], conv_b=conv_b[l], lru_w_a=lru_w_a[l], lru_b_a=lru_b_a[l],
            lru_w_i=lru_w_i[l], lru_b_i=lru_b_i[l], lru_lambda=lru_lambda[l],
            w_attn_out=w_attn_out[l].astype(BF16), w_rnn_out=w_rnn_out[l].astype(BF16),
            b_gate=b_gate[l], w_out=w_out[l].astype(BF16), norm_mlp=norm_mlp[l],
            w_up=w_up[l].astype(BF16), w_down=w_down[l].astype(BF16)))
    y_prompt = _trunk(x_prompt, layers, norm_final)
    y_sample = _trunk(x_sample, layers, norm_final)
    return (y_prompt, y_sample)
```

```python
import functools
import math

import jax
import jax.numpy as jnp
from jax import lax
from jax.experimental import pallas as pl
from jax.experimental.pallas import tpu as pltpu

F32 = jnp.float32
BF16 = jnp.bfloat16

HEAD_DIM = 128
N_Q_HEADS = 16
N_KV_HEADS = 4
GQA_GROUP = N_Q_HEADS // N_KV_HEADS
ATTN_WIDTH = N_Q_HEADS * HEAD_DIM
KV_WIDTH = N_KV_HEADS * HEAD_DIM
ROPE_THETA = 10000.0
GRID_W = 64
LRU_C = 8.0
CONV_W = 4
NORM_EPS = 1e-6
LANES = 128
HALO = 16
V7X_VMEM_LIMIT = 56 * 1024 * 1024


def _params(n_axes, vmem=V7X_VMEM_LIMIT):
    return pltpu.CompilerParams(
        dimension_semantics=("arbitrary",) * n_axes, vmem_limit_bytes=vmem)


def _tile(n, want):
    t = min(n, want)
    while n % t:
        t //= 2
    return t


def _rmsnorm_cast_kernel(x_ref, g_ref, o_ref):
    x = x_ref[...]
    ms = jnp.mean(x * x, axis=-1, keepdims=True)
    o_ref[...] = (x * lax.rsqrt(ms + NORM_EPS) * g_ref[...]).astype(o_ref.dtype)


def _rmsnorm_cast(x, g):
    t, d = x.shape
    tr = _tile(t, 256)
    return pl.pallas_call(
        _rmsnorm_cast_kernel,
        grid=(t // tr,),
        in_specs=[pl.BlockSpec((tr, d), lambda i: (i, 0)),
                  pl.BlockSpec((1, d), lambda i: (0, 0))],
        out_specs=pl.BlockSpec((tr, d), lambda i: (i, 0)),
        out_shape=jax.ShapeDtypeStruct((t, d), BF16),
        compiler_params=_params(1),
        name="rmsnorm_cast",
    )(x, g.reshape(1, d))


def _mm_kernel(a_ref, w_ref, o_ref):
    o_ref[...] = jnp.dot(a_ref[...], w_ref[...],
                         preferred_element_type=F32).astype(o_ref.dtype)


def _matmul(a, w, name):
    m, k = a.shape
    n = w.shape[1]
    tm, tn = _tile(m, 1024), _tile(n, 1024)
    return pl.pallas_call(
        _mm_kernel,
        grid=(m // tm, n // tn),
        in_specs=[pl.BlockSpec((tm, k), lambda i, j: (i, 0)),
                  pl.BlockSpec((k, tn), lambda i, j: (0, j))],
        out_specs=pl.BlockSpec((tm, tn), lambda i, j: (i, j)),
        out_shape=jax.ShapeDtypeStruct((m, n), BF16),
        compiler_params=_params(2),
        name=name,
    )(a, w)


def _mlp_up_kernel(a_ref, w_ref, o_ref):
    y = jnp.dot(a_ref[...], w_ref[...], preferred_element_type=F32)
    r = jnp.maximum(y, 0.0)
    o_ref[...] = (r * r).astype(o_ref.dtype)


def _mlp_up(a, w):
    m, k = a.shape
    n = w.shape[1]
    tm, tn = _tile(m, 1024), _tile(n, 1024)
    return pl.pallas_call(
        _mlp_up_kernel,
        grid=(m // tm, n // tn),
        in_specs=[pl.BlockSpec((tm, k), lambda i, j: (i, 0)),
                  pl.BlockSpec((k, tn), lambda i, j: (0, j))],
        out_specs=pl.BlockSpec((tm, tn), lambda i, j: (i, j)),
        out_shape=jax.ShapeDtypeStruct((m, n), BF16),
        compiler_params=_params(2),
        name="mlp_up",
    )(a, w)


def _merge_kernel(attn_ref, rec_ref, wa_ref, wr_ref, ga_ref, gr_ref, ba_ref, br_ref, o_ref):
    ab = jnp.dot(attn_ref[...], wa_ref[...], preferred_element_type=F32)
    rb = jnp.dot(rec_ref[...], wr_ref[...], preferred_element_type=F32)
    ga = jax.nn.sigmoid(ga_ref[...].astype(F32) + ba_ref[...])
    gr = jax.nn.sigmoid(gr_ref[...].astype(F32) + br_ref[...])
    o_ref[...] = (ga * ab + gr * rb).astype(o_ref.dtype)


def _merge(attn, rec, wa, wr, z, b_gate, gate_col0):
    m, ka = attn.shape
    kr = rec.shape[1]
    n = wa.shape[1]
    tm, tn = _tile(m, 1024), _tile(n, 512)
    ga_blk = gate_col0 // tn
    gr_blk = (gate_col0 + n) // tn
    return pl.pallas_call(
        _merge_kernel,
        grid=(m // tm, n // tn),
        in_specs=[pl.BlockSpec((tm, ka), lambda i, j: (i, 0)),
                  pl.BlockSpec((tm, kr), lambda i, j: (i, 0)),
                  pl.BlockSpec((ka, tn), lambda i, j: (0, j)),
                  pl.BlockSpec((kr, tn), lambda i, j: (0, j)),
                  pl.BlockSpec((tm, tn), lambda i, j: (i, ga_blk + j)),
                  pl.BlockSpec((tm, tn), lambda i, j: (i, gr_blk + j)),
                  pl.BlockSpec((1, tn), lambda i, j: (0, j)),
                  pl.BlockSpec((1, tn), lambda i, j: (0, j))],
        out_specs=pl.BlockSpec((tm, tn), lambda i, j: (i, j)),
        out_shape=jax.ShapeDtypeStruct((m, n), BF16),
        compiler_params=_params(2),
        name="branch_merge",
    )(attn, rec, wa, wr, z, z, b_gate[0:1], b_gate[1:2])


def _outproj_kernel(a_ref, w_ref, x_ref, o_ref):
    o_ref[...] = x_ref[...] + jnp.dot(a_ref[...], w_ref[...], preferred_element_type=F32)


def _outproj(a, w, x):
    m, k = a.shape
    n = w.shape[1]
    tm, tn = _tile(m, 1024), _tile(n, 512)
    return pl.pallas_call(
        _outproj_kernel,
        grid=(m // tm, n // tn),
        in_specs=[pl.BlockSpec((tm, k), lambda i, j: (i, 0)),
                  pl.BlockSpec((k, tn), lambda i, j: (0, j)),
                  pl.BlockSpec((tm, tn), lambda i, j: (i, j))],
        out_specs=pl.BlockSpec((tm, tn), lambda i, j: (i, j)),
        out_shape=jax.ShapeDtypeStruct((m, n), F32),
        compiler_params=_params(2),
        name="out_proj",
    )(a, w, x)


def _mlp_down_kernel(u_ref, w_ref, x_ref, g_ref, o_ref, *, final_norm):
    kk = pl.program_id(1)

    @pl.when(kk == 0)
    def _():
        o_ref[...] = x_ref[...]

    o_ref[...] += jnp.dot(u_ref[...], w_ref[...], preferred_element_type=F32)

    if final_norm:
        @pl.when(kk == pl.num_programs(1) - 1)
        def _():
            x = o_ref[...]
            ms = jnp.mean(x * x, axis=-1, keepdims=True)
            o_ref[...] = x * lax.rsqrt(ms + NORM_EPS) * g_ref[...]


def _mlp_down(u, w, x, g, final_norm):
    m, k = u.shape
    n = w.shape[1]
    tm, tk = _tile(m, 512), _tile(k, 1024)
    return pl.pallas_call(
        functools.partial(_mlp_down_kernel, final_norm=final_norm),
        grid=(m // tm, k // tk),
        in_specs=[pl.BlockSpec((tm, tk), lambda i, kk: (i, kk)),
                  pl.BlockSpec((tk, n), lambda i, kk: (kk, 0)),
                  pl.BlockSpec((tm, n), lambda i, kk: (i, 0), pipeline_mode=pl.Buffered(1)),
                  pl.BlockSpec((1, n), lambda i, kk: (0, 0))],
        out_specs=pl.BlockSpec((tm, n), lambda i, kk: (i, 0)),
        out_shape=jax.ShapeDtypeStruct((m, n), F32),
        compiler_params=_params(2),
        name="mlp_down_norm",
    )(u, w, x, g.reshape(1, n))


Q_SCALE = math.log2(math.e) * HEAD_DIM ** -0.5


def _rope_tables(seq_len):
    axis_dim = HEAD_DIM // 2
    n_rows = seq_len // GRID_W
    row = jnp.repeat(jnp.arange(n_rows), GRID_W).astype(F32)
    col = jnp.tile(jnp.arange(GRID_W), n_rows).astype(F32)
    inv = ROPE_THETA ** (-jnp.arange(0, axis_dim, 2, dtype=F32) / axis_dim)
    ang = jnp.concatenate([row[:, None] * inv, col[:, None] * inv], axis=-1)
    cos, sin = jnp.cos(ang), jnp.sin(ang)
    cosf = jnp.repeat(cos, 2, axis=-1)
    sinf = jnp.stack([-sin, sin], axis=-1).reshape(seq_len, HEAD_DIM)
    return cosf, sinf


def _rope_kernel(q_ref, k_ref, v_ref, cos_ref, sin_ref, qg_ref, kg_ref, qt_ref, ko_ref, vt_ref):
    cos = cos_ref[...]
    sin = sin_ref[...]
    lane = lax.broadcasted_iota(jnp.int32, cos.shape, 1)
    even = (lane & 1) == 0

    def norm_rope(x, g):
        x = x.astype(F32)
        ms = jnp.mean(x * x, axis=-1, keepdims=True)
        xn = x * lax.rsqrt(ms + NORM_EPS) * g
        nxt = pltpu.roll(xn, HEAD_DIM - 1, 1)
        prv = pltpu.roll(xn, 1, 1)
        return xn * cos + jnp.where(even, nxt, prv) * sin

    qg = qg_ref[...]
    kg = kg_ref[...]
    for h in range(N_Q_HEADS):
        sl = slice(h * HEAD_DIM, (h + 1) * HEAD_DIM)
        qt_ref[sl, :] = (norm_rope(q_ref[:, sl], qg) * Q_SCALE).T.astype(qt_ref.dtype)
    for h in range(N_KV_HEADS):
        sl = slice(h * HEAD_DIM, (h + 1) * HEAD_DIM)
        ko_ref[:, sl] = norm_rope(k_ref[:, sl], kg).astype(ko_ref.dtype)
        vt_ref[sl, :] = v_ref[:, sl].astype(F32).T.astype(vt_ref.dtype)


def _rope(z3, q_norm, k_norm):
    b, s, _ = z3.shape
    tr = _tile(s, 256)
    cosf, sinf = _rope_tables(s)
    k_blk = ATTN_WIDTH // KV_WIDTH
    return pl.pallas_call(
        _rope_kernel,
        grid=(b, s // tr),
        in_specs=[pl.BlockSpec((None, tr, ATTN_WIDTH), lambda bi, i: (bi, i, 0)),
                  pl.BlockSpec((None, tr, KV_WIDTH), lambda bi, i: (bi, i, k_blk)),
                  pl.BlockSpec((None, tr, KV_WIDTH), lambda bi, i: (bi, i, k_blk + 1)),
                  pl.BlockSpec((tr, HEAD_DIM), lambda bi, i: (i, 0)),
                  pl.BlockSpec((tr, HEAD_DIM), lambda bi, i: (i, 0)),
                  pl.BlockSpec((1, HEAD_DIM), lambda bi, i: (0, 0)),
                  pl.BlockSpec((1, HEAD_DIM), lambda bi, i: (0, 0))],
        out_specs=[pl.BlockSpec((None, ATTN_WIDTH, tr), lambda bi, i: (bi, 0, i)),
                   pl.BlockSpec((None, tr, KV_WIDTH), lambda bi, i: (bi, i, 0)),
                   pl.BlockSpec((None, KV_WIDTH, tr), lambda bi, i: (bi, 0, i))],
        out_shape=[jax.ShapeDtypeStruct((b, ATTN_WIDTH, s), BF16),
                   jax.ShapeDtypeStruct((b, s, KV_WIDTH), BF16),
                   jax.ShapeDtypeStruct((b, KV_WIDTH, s), BF16)],
        compiler_params=_params(2),
        name="qk_norm_rope",
    )(z3, z3, z3, cosf, sinf, q_norm.reshape(1, HEAD_DIM), k_norm.reshape(1, HEAD_DIM))


MAX_STATIC_PAIRS = 8


def _attn_kernel(qt_ref, qn_ref, k_ref, vt_ref, o_ref, qt_scr, st_scr, p_scr, acc_scr, *, tk):
    tq = qt_ref.shape[1]
    n_pairs = k_ref.shape[0] // (2 * tk)
    n_chunks = 2 * n_pairs

    def stage_queries(src_ref):
        for g in range(GQA_GROUP):
            qt_scr[:, g * tq:(g + 1) * tq] = src_ref[g * HEAD_DIM:(g + 1) * HEAD_DIM, :]

    def scores(c, slot):
        start = pl.multiple_of(c * tk, tk)
        st_scr[slot] = jnp.dot(k_ref[pl.ds(start, tk), :], qt_scr[...],
                               preferred_element_type=F32)

    def weighted_values(c, slot, alpha):
        start = pl.multiple_of(c * tk, tk)
        acc_scr[...] = alpha * acc_scr[...] + jnp.dot(
            vt_ref[:, pl.ds(start, tk)], p_scr[slot], preferred_element_type=F32)

    def softmax(slot, m_old, l_old):
        st = st_scr[slot]
        m_new = jnp.maximum(m_old, jnp.max(st, axis=0, keepdims=True))
        alpha = jnp.exp2(m_old - m_new)
        p = jnp.exp2(st - m_new)
        p_scr[slot] = p.astype(p_scr.dtype)
        return m_new, alpha * l_old + jnp.sum(p, axis=0, keepdims=True), alpha

    def step(c, slot, carry, has_prev):
        m, l, alpha_prev = carry
        if has_prev:
            weighted_values(c - 1, 1 - slot, alpha_prev)
        if isinstance(c, int) and c == n_chunks - 1:
            stage_queries(qn_ref)
            scores(0, 0)
        else:
            scores(c + 1, 1 - slot)
        return softmax(slot, m, l)

    def pair(c, carry, first):
        carry = step(c, 0, carry, not first)
        return step(c + 1, 1, carry, True)

    @pl.when(pl.program_id(2) == 0)
    def _():
        stage_queries(qt_ref)
        scores(0, 0)

    acc_scr[...] = jnp.zeros(acc_scr.shape, F32)
    rows = acc_scr.shape[1]
    zero = jnp.zeros((1, rows), F32)
    carry = (jnp.full((1, rows), -jnp.inf, F32), zero, zero)
    if n_pairs <= MAX_STATIC_PAIRS:
        for j in range(n_pairs):
            carry = pair(2 * j, carry, j == 0)
    else:
        carry = pair(0, carry, True)
        carry = lax.fori_loop(1, n_pairs - 1, lambda j, cr: pair(2 * j, cr, False), carry)
        carry = pair(n_chunks - 2, carry, False)
    _, l_fin, alpha_last = carry
    weighted_values(n_chunks - 1, 1, alpha_last)
    out = acc_scr[...] * (1.0 / l_fin)
    for g in range(GQA_GROUP):
        o_ref[:, g * HEAD_DIM:(g + 1) * HEAD_DIM] = out[:, g * tq:(g + 1) * tq].T.astype(o_ref.dtype)


def _attention(qt, kr, vt):
    b, s, _ = kr.shape
    tq = _tile(s, 256)
    tk = _tile(s // 2, 512)
    rows = GQA_GROUP * tq
    gw = GQA_GROUP * HEAD_DIM
    n_tiles = s // tq
    return pl.pallas_call(
        functools.partial(_attn_kernel, tk=tk),
        grid=(b, N_KV_HEADS, n_tiles),
        in_specs=[pl.BlockSpec((None, gw, tq), lambda bi, h, i: (bi, h, i)),
                  pl.BlockSpec((None, gw, tq),
                               lambda bi, h, i: (bi, h, jnp.minimum(i + 1, n_tiles - 1))),
                  pl.BlockSpec((None, s, HEAD_DIM), lambda bi, h, i: (bi, 0, h)),
                  pl.BlockSpec((None, HEAD_DIM, s), lambda bi, h, i: (bi, h, 0))],
        out_specs=pl.BlockSpec((None, tq, gw), lambda bi, h, i: (bi, i, h)),
        out_shape=jax.ShapeDtypeStruct((b, s, ATTN_WIDTH), BF16),
        scratch_shapes=[pltpu.VMEM((HEAD_DIM, rows), BF16),
                        pltpu.VMEM((2, tk, rows), F32),
                        pltpu.VMEM((2, tk, rows), BF16),
                        pltpu.VMEM((HEAD_DIM, rows), F32)],
        compiler_params=_params(3),
        name="gqa_attention",
    )(qt, qt, kr, vt)


def _softplus(x):
    return jnp.maximum(x, 0.0) + jnp.log1p(jnp.exp(-jnp.abs(x)))


def _gelu_tanh(x):
    c = math.sqrt(2.0 / math.pi)
    return 0.5 * x * (1.0 + jnp.tanh(c * (x + 0.044715 * (x * x * x))))


def _lru_kernel(xr_ref, yr_ref, cw_ref, cb_ref, wg_ref, bg_ref, lam_ref, o_ref,
                xpad, hf, hb, *, chunk):
    s = xr_ref.shape[0]
    n_chunks = s // chunk
    n_tiles = chunk // 8
    w = LANES

    xpad[0:HALO, :] = jnp.zeros((HALO, w), F32)
    xpad[s + HALO:s + 2 * HALO, :] = jnp.zeros((HALO, w), F32)

    def fill(c, carry):
        t0 = pl.multiple_of(c * chunk, chunk)
        xpad[pl.ds(t0 + HALO, chunk), :] = xr_ref[pl.ds(t0, chunk), :].astype(F32)
        return carry

    lax.fori_loop(0, n_chunks, fill, 0)

    cw = cw_ref[...]
    cb = cb_ref[...]
    sp = _softplus(-lam_ref[...])
    sub = lax.broadcasted_iota(jnp.int32, (1, 8, w), 1)
    ext = chunk + 2 * HALO

    def conv(t0):
        e = xpad[pl.ds(t0, ext), :]
        out = cb + cw[0:1] * pltpu.roll(e, 1, 0)[HALO:HALO + chunk]
        out = out + cw[1:2] * e[HALO:HALO + chunk]
        out = out + cw[2:3] * pltpu.roll(e, ext - 1, 0)[HALO:HALO + chunk]
        out = out + cw[3:4] * pltpu.roll(e, ext - 2, 0)[HALO:HALO + chunk]
        return out

    def gate_terms(xc, direction):
        col = 2 * w * direction
        g = jnp.dot(xc.astype(BF16), wg_ref[:, col:col + 2 * w],
                    preferred_element_type=F32) + bg_ref[:, col:col + 2 * w]
        r = jax.nn.sigmoid(g[:, :w])
        i = jax.nn.sigmoid(g[:, w:])
        log_a = (-LRU_C * r) * sp[direction:direction + 1]
        a = jnp.exp(log_a)
        u = jnp.sqrt(-jnp.tanh(log_a) * (a * a + 1.0)) * (i * xc)
        return a.reshape(n_tiles, 8, w), u.reshape(n_tiles, 8, w)

    def tile_scan(a, u, reverse):
        for d in (1, 2, 4):
            shift = (8 - d) if reverse else d
            keep = (sub < 8 - d) if reverse else (sub >= d)
            a_sh = pltpu.roll(a, shift, 1)
            u_sh = pltpu.roll(u, shift, 1)
            u = jnp.where(keep, u + a * u_sh, u)
            a = jnp.where(keep, a * a_sh, a)
        return a, u

    def body(c, carry):
        h_f, h_b = carry
        tf = pl.multiple_of(c * chunk, chunk)
        tb = pl.multiple_of((n_chunks - 1 - c) * chunk, chunk)
        a_f, u_f = tile_scan(*gate_terms(conv(tf), 0), reverse=False)
        a_b, u_b = tile_scan(*gate_terms(conv(tb), 1), reverse=True)
        for k in range(n_tiles):
            ht = u_f[k] + a_f[k] * h_f
            hf[pl.ds(tf + 8 * k, 8), :] = ht
            h_f = jnp.broadcast_to(ht[7:8, :], (8, w))
            kb = n_tiles - 1 - k
            ht = u_b[kb] + a_b[kb] * h_b
            hb[pl.ds(tb + 8 * kb, 8), :] = ht
            h_b = jnp.broadcast_to(ht[0:1, :], (8, w))
        return h_f, h_b

    zero = jnp.zeros((8, w), F32)
    lax.fori_loop(0, n_chunks, body, (zero, zero))

    def emit(c, carry):
        t0 = pl.multiple_of(c * chunk, chunk)
        y = yr_ref[pl.ds(t0, chunk), :].astype(F32)
        rec = (hf[pl.ds(t0, chunk), :] + hb[pl.ds(t0, chunk), :]) * _gelu_tanh(y)
        o_ref[pl.ds(t0, chunk), :] = rec.astype(o_ref.dtype)
        return carry

    lax.fori_loop(0, n_chunks, emit, 0)


def _lru(z3, conv_w, conv_b, w_a, b_a, w_i, b_i, lam, x_col0, y_col0):
    b, s, _ = z3.shape
    n_blocks = w_a.shape[1]
    d_rnn = n_blocks * LANES
    chunk = _tile(s, 256)
    wg = jnp.concatenate([w_a[0], w_i[0], w_a[1], w_i[1]], axis=-1).astype(BF16)
    bg = jnp.concatenate(
        [v.reshape(n_blocks, 1, LANES) for v in (b_a[0], b_i[0], b_a[1], b_i[1])], axis=-1)
    x_blk = x_col0 // LANES
    y_blk = y_col0 // LANES
    return pl.pallas_call(
        functools.partial(_lru_kernel, chunk=chunk),
        grid=(b, n_blocks),
        in_specs=[pl.BlockSpec((None, s, LANES), lambda bi, j: (bi, 0, x_blk + j)),
                  pl.BlockSpec((None, s, LANES), lambda bi, j: (bi, 0, y_blk + j)),
                  pl.BlockSpec((CONV_W, LANES), lambda bi, j: (0, j)),
                  pl.BlockSpec((1, LANES), lambda bi, j: (0, j)),
                  pl.BlockSpec((None, LANES, 4 * LANES), lambda bi, j: (j, 0, 0)),
                  pl.BlockSpec((None, 1, 4 * LANES), lambda bi, j: (j, 0, 0)),
                  pl.BlockSpec((2, LANES), lambda bi, j: (0, j))],
        out_specs=pl.BlockSpec((None, s, LANES), lambda bi, j: (bi, 0, j)),
        out_shape=jax.ShapeDtypeStruct((b, s, d_rnn), BF16),
        scratch_shapes=[pltpu.VMEM((s + 2 * HALO, LANES), F32),
                        pltpu.VMEM((s, LANES), F32),
                        pltpu.VMEM((s, LANES), F32)],
        compiler_params=_params(2),
        name="conv_rglru",
    )(z3, z3, conv_w, conv_b.reshape(1, d_rnn), wg, bg, lam)


def _encoder_layer(x2, b, s, p):
    d = x2.shape[1]
    d_rnn = p["w_rnn_out"].shape[0]
    c_k = ATTN_WIDTH
    c_v = c_k + KV_WIDTH
    c_x = c_v + KV_WIDTH
    c_y = c_x + d_rnn
    c_g = c_y + d_rnn

    h = _rmsnorm_cast(x2, p["norm_mix"])
    z = _matmul(h, p["w_in"], "in_proj")
    z3 = z.reshape(b, s, z.shape[1])

    qt, kr, vt = _rope(z3, p["q_norm"], p["k_norm"])
    attn = _attention(qt, kr, vt)
    rec = _lru(z3, p["conv_w"], p["conv_b"], p["lru_w_a"], p["lru_b_a"], p["lru_w_i"],
               p["lru_b_i"], p["lru_lambda"], c_x, c_y)

    merged = _merge(attn.reshape(b * s, ATTN_WIDTH), rec.reshape(b * s, d_rnn),
                    p["w_attn_out"], p["w_rnn_out"], z, p["b_gate"], c_g)
    x1 = _outproj(merged, p["w_out"], x2)
    hm = _rmsnorm_cast(x1, p["norm_mlp"])
    up = _mlp_up(hm, p["w_up"])
    return up, x1


def _trunk(x, layers, norm_final):
    b, s, d = x.shape
    x2 = x.reshape(b * s, d)
    for li, p in enumerate(layers):
        up, x1 = _encoder_layer(x2, b, s, p)
        x2 = _mlp_down(up, p["w_down"], x1, norm_final, final_norm=(li == len(layers) - 1))
    return x2.reshape(b, s, d)


def kernel(x_prompt, x_sample, norm_mix, w_in, q_norm, k_norm, conv_w, conv_b, lru_w_a, lru_b_a,
           lru_w_i, lru_b_i, lru_lambda, w_attn_out, w_rnn_out, b_gate, w_out, norm_mlp, w_up,
           w_down, norm_final):
    depth = w_in.shape[0]
    layers = []
    for l in range(depth):
        layers.append(dict(
            norm_mix=norm_mix[l], w_in=w_in[l].astype(BF16), q_norm=q_norm[l], k_norm=k_norm[l],
            conv_w=conv_w[l], conv_b=conv_b[l], lru_w_a=lru_w_a[l], lru_b_a=lru_b_a[l],
            lru_w_i=lru_w_i[l], lru_b_i=lru_b_i[l], lru_lambda=lru_lambda[l],
            w_attn_out=w_attn_out[l].astype(BF16), w_rnn_out=w_rnn_out[l].astype(BF16),
            b_gate=b_gate[l], w_out=w_out[l].astype(BF16), norm_mlp=norm_mlp[l],
            w_up=w_up[l].astype(BF16), w_down=w_down[l].astype(BF16)))
    y_prompt = _trunk(x_prompt, layers, norm_final)
    y_sample = _trunk(x_sample, layers, norm_final)
    return (y_prompt, y_sample)
```

```python
import functools
import math

import jax
import jax.numpy as jnp
from jax import lax
from jax.experimental import pallas as pl
from jax.experimental.pallas import tpu as pltpu

F32 = jnp.float32
BF16 = jnp.bfloat16

HEAD_DIM = 128
N_Q_HEADS = 16
N_KV_HEADS = 4
GQA_GROUP = N_Q_HEADS // N_KV_HEADS
ATTN_WIDTH = N_Q_HEADS * HEAD_DIM
KV_WIDTH = N_KV_HEADS * HEAD_DIM
ROPE_THETA = 10000.0
GRID_W = 64
LRU_C = 8.0
CONV_W = 4
NORM_EPS = 1e-6
LANES = 128
MXU_COLS = 256
HALO = 16
V7X_VMEM_LIMIT = 60 * 1024 * 1024


def _params(n_axes, vmem=V7X_VMEM_LIMIT):
    return pltpu.CompilerParams(
        dimension_semantics=("arbitrary",) * n_axes, vmem_limit_bytes=vmem)


def _tile(n, want):
    t = min(n, want)
    while n % t:
        t //= 2
    return t


def _rmsnorm_cast_kernel(x_ref, g_ref, o_ref):
    x = x_ref[...]
    ms = jnp.mean(x * x, axis=-1, keepdims=True)
    o_ref[...] = (x * lax.rsqrt(ms + NORM_EPS) * g_ref[...]).astype(o_ref.dtype)


def _rmsnorm_cast(x, g):
    t, d = x.shape
    tr = _tile(t, 256)
    return pl.pallas_call(
        _rmsnorm_cast_kernel,
        grid=(t // tr,),
        in_specs=[pl.BlockSpec((tr, d), lambda i: (i, 0)),
                  pl.BlockSpec((1, d), lambda i: (0, 0))],
        out_specs=pl.BlockSpec((tr, d), lambda i: (i, 0)),
        out_shape=jax.ShapeDtypeStruct((t, d), BF16),
        compiler_params=_params(1),
        name="rmsnorm_cast",
    )(x, g.reshape(1, d))


def _mlp_up_kernel(a_ref, w_ref, o_ref):
    y = jnp.dot(a_ref[...], w_ref[...], preferred_element_type=F32)
    r = jnp.maximum(y, 0.0)
    o_ref[...] = (r * r).astype(o_ref.dtype)


def _mlp_up(a, w):
    m, k = a.shape
    n = w.shape[1]
    tm, tn = _tile(m, 1024), _tile(n, 1024)
    return pl.pallas_call(
        _mlp_up_kernel,
        grid=(m // tm, n // tn),
        in_specs=[pl.BlockSpec((tm, k), lambda i, j: (i, 0)),
                  pl.BlockSpec((k, tn), lambda i, j: (0, j))],
        out_specs=pl.BlockSpec((tm, tn), lambda i, j: (i, j)),
        out_shape=jax.ShapeDtypeStruct((m, n), BF16),
        compiler_params=_params(2),
        name="mlp_up",
    )(a, w)


def _merge_kernel(attn_ref, rec_ref, wa_ref, wr_ref, ga_ref, gr_ref, ba_ref, br_ref, o_ref):
    ab = jnp.dot(attn_ref[...], wa_ref[...], preferred_element_type=F32)
    rb = jnp.dot(rec_ref[...], wr_ref[...], preferred_element_type=F32)
    ga = jax.nn.sigmoid(ga_ref[...].astype(F32) + ba_ref[...])
    gr = jax.nn.sigmoid(gr_ref[...].astype(F32) + br_ref[...])
    o_ref[...] = (ga * ab + gr * rb).astype(o_ref.dtype)


def _merge(attn, rec, wa, wr, z, b_gate, gate_col0):
    m, ka = attn.shape
    kr = rec.shape[1]
    n = wa.shape[1]
    tm, tn = _tile(m, 1024), _tile(n, 512)
    ga_blk = gate_col0 // tn
    gr_blk = (gate_col0 + n) // tn
    return pl.pallas_call(
        _merge_kernel,
        grid=(m // tm, n // tn),
        in_specs=[pl.BlockSpec((tm, ka), lambda i, j: (i, 0)),
                  pl.BlockSpec((tm, kr), lambda i, j: (i, 0)),
                  pl.BlockSpec((ka, tn), lambda i, j: (0, j)),
                  pl.BlockSpec((kr, tn), lambda i, j: (0, j)),
                  pl.BlockSpec((tm, tn), lambda i, j: (i, ga_blk + j)),
                  pl.BlockSpec((tm, tn), lambda i, j: (i, gr_blk + j)),
                  pl.BlockSpec((1, tn), lambda i, j: (0, j)),
                  pl.BlockSpec((1, tn), lambda i, j: (0, j))],
        out_specs=pl.BlockSpec((tm, tn), lambda i, j: (i, j)),
        out_shape=jax.ShapeDtypeStruct((m, n), BF16),
        compiler_params=_params(2),
        name="branch_merge",
    )(attn, rec, wa, wr, z, z, b_gate[0:1], b_gate[1:2])


def _outproj_kernel(a_ref, w_ref, x_ref, o_ref):
    o_ref[...] = x_ref[...] + jnp.dot(a_ref[...], w_ref[...], preferred_element_type=F32)


def _outproj(a, w, x):
    m, k = a.shape
    n = w.shape[1]
    tm, tn = _tile(m, 1024), _tile(n, 512)
    return pl.pallas_call(
        _outproj_kernel,
        grid=(m // tm, n // tn),
        in_specs=[pl.BlockSpec((tm, k), lambda i, j: (i, 0)),
                  pl.BlockSpec((k, tn), lambda i, j: (0, j)),
                  pl.BlockSpec((tm, tn), lambda i, j: (i, j))],
        out_specs=pl.BlockSpec((tm, tn), lambda i, j: (i, j)),
        out_shape=jax.ShapeDtypeStruct((m, n), F32),
        compiler_params=_params(2),
        name="out_proj",
    )(a, w, x)


def _mlp_down_kernel(u_ref, w_ref, x_ref, g_ref, o_ref, *, final_norm):
    kk = pl.program_id(1)

    @pl.when(kk == 0)
    def _():
        o_ref[...] = x_ref[...]

    o_ref[...] += jnp.dot(u_ref[...], w_ref[...], preferred_element_type=F32)

    if final_norm:
        @pl.when(kk == pl.num_programs(1) - 1)
        def _():
            x = o_ref[...]
            ms = jnp.mean(x * x, axis=-1, keepdims=True)
            o_ref[...] = x * lax.rsqrt(ms + NORM_EPS) * g_ref[...]


def _mlp_down(u, w, x, g, final_norm):
    m, k = u.shape
    n = w.shape[1]
    tm, tk = _tile(m, 512), _tile(k, 1024)
    return pl.pallas_call(
        functools.partial(_mlp_down_kernel, final_norm=final_norm),
        grid=(m // tm, k // tk),
        in_specs=[pl.BlockSpec((tm, tk), lambda i, kk: (i, kk)),
                  pl.BlockSpec((tk, n), lambda i, kk: (kk, 0)),
                  pl.BlockSpec((tm, n), lambda i, kk: (i, 0), pipeline_mode=pl.Buffered(1)),
                  pl.BlockSpec((1, n), lambda i, kk: (0, 0))],
        out_specs=pl.BlockSpec((tm, n), lambda i, kk: (i, 0)),
        out_shape=jax.ShapeDtypeStruct((m, n), F32),
        compiler_params=_params(2),
        name="mlp_down_norm",
    )(u, w, x, g.reshape(1, n))


Q_SCALE = math.log2(math.e) * HEAD_DIM ** -0.5


def _split_pairs(v):
    lead = v.shape[:-1]
    v = v.reshape(lead + (v.shape[-1] // HEAD_DIM, HEAD_DIM // 2, 2))
    return jnp.swapaxes(v, -1, -2).reshape(lead + (-1,))


def _rope_tables(seq_len):
    axis_dim = HEAD_DIM // 2
    n_rows = seq_len // GRID_W
    row = jnp.repeat(jnp.arange(n_rows), GRID_W).astype(F32)
    col = jnp.tile(jnp.arange(GRID_W), n_rows).astype(F32)
    inv = ROPE_THETA ** (-jnp.arange(0, axis_dim, 2, dtype=F32) / axis_dim)
    ang = jnp.concatenate([row[:, None] * inv, col[:, None] * inv], axis=-1)
    cos, sin = jnp.cos(ang), jnp.sin(ang)
    return jnp.concatenate([cos, cos], axis=-1), jnp.concatenate([-sin, sin], axis=-1)


IN_PROJ_TN = 2 * KV_WIDTH


def _norm_rope(x, g, cos, sin):
    ms = jnp.mean(x * x, axis=-1, keepdims=True)
    xn = x * lax.rsqrt(ms + NORM_EPS) * g
    return xn * cos + pltpu.roll(xn, HEAD_DIM // 2, 1) * sin


def _in_proj_kernel(a_ref, w_ref, cos_ref, sin_ref, kg_ref, ko_ref, vt_ref, z_ref, *, kv_tile):
    j = pl.program_id(1)

    def product():
        return jnp.dot(a_ref[...], w_ref[...], preferred_element_type=F32)

    @pl.when(j == kv_tile)
    def _():
        y = product()
        for h in range(N_KV_HEADS):
            sl = slice(h * HEAD_DIM, (h + 1) * HEAD_DIM)
            ko_ref[:, sl] = _norm_rope(y[:, sl], kg_ref[...], cos_ref[...],
                                       sin_ref[...]).astype(ko_ref.dtype)
            vsl = slice(KV_WIDTH + h * HEAD_DIM, KV_WIDTH + (h + 1) * HEAD_DIM)
            vt_ref[sl, :] = y[:, vsl].T.astype(vt_ref.dtype)

    @pl.when(j != kv_tile)
    def _():
        z_ref[...] = product().astype(z_ref.dtype)


def _in_proj(h, w, k_norm, cosf, sinf, b, s):
    t, k = h.shape
    n = w.shape[1]
    tn = IN_PROJ_TN
    tm = _tile(s, 1024)
    assert ATTN_WIDTH % tn == 0 and n % tn == 0
    kv_tile = ATTN_WIDTH // tn
    tps = s // tm

    def z_block(i, j):
        return i, jnp.where(j < kv_tile, j, j - 1)

    return pl.pallas_call(
        functools.partial(_in_proj_kernel, kv_tile=kv_tile),
        grid=(t // tm, n // tn),
        in_specs=[pl.BlockSpec((tm, k), lambda i, j: (i, 0)),
                  pl.BlockSpec((k, tn), lambda i, j: (0, j)),
                  pl.BlockSpec((tm, HEAD_DIM), lambda i, j: (i % tps, 0)),
                  pl.BlockSpec((tm, HEAD_DIM), lambda i, j: (i % tps, 0)),
                  pl.BlockSpec((1, HEAD_DIM), lambda i, j: (0, 0))],
        out_specs=[pl.BlockSpec((None, tm, KV_WIDTH), lambda i, j: (i // tps, i % tps, 0)),
                   pl.BlockSpec((None, KV_WIDTH, tm), lambda i, j: (i // tps, 0, i % tps)),
                   pl.BlockSpec((tm, tn), z_block)],
        out_shape=[jax.ShapeDtypeStruct((b, s, KV_WIDTH), BF16),
                   jax.ShapeDtypeStruct((b, KV_WIDTH, s), BF16),
                   jax.ShapeDtypeStruct((t, n - tn), BF16)],
        compiler_params=_params(2),
        name="in_proj",
    )(h, w, cosf, sinf, k_norm.reshape(1, HEAD_DIM))


MAX_STATIC_PAIRS = 8


def _attn_kernel(q_ref, qn_ref, cos_ref, sin_ref, cosn_ref, sinn_ref, qg_ref, k_ref, vt_ref, o_ref,
                 qt_scr, st_scr, p_scr, acc_scr, *, tk):
    tq = q_ref.shape[0]
    n_pairs = k_ref.shape[0] // (2 * tk)
    n_chunks = 2 * n_pairs

    def stage_queries(src_ref, c_ref, s_ref):
        for g in range(GQA_GROUP):
            x = src_ref[:, g * HEAD_DIM:(g + 1) * HEAD_DIM].astype(F32)
            q = _norm_rope(x, qg_ref[...], c_ref[...], s_ref[...]) * Q_SCALE
            qt_scr[:, g * tq:(g + 1) * tq] = q.T.astype(qt_scr.dtype)

    def scores(c, slot):
        start = pl.multiple_of(c * tk, tk)
        st_scr[slot] = jnp.dot(k_ref[pl.ds(start, tk), :], qt_scr[...],
                               preferred_element_type=F32)

    def weighted_values(c, slot, alpha):
        start = pl.multiple_of(c * tk, tk)
        acc_scr[...] = alpha * acc_scr[...] + jnp.dot(
            vt_ref[:, pl.ds(start, tk)], p_scr[slot], preferred_element_type=F32)

    def softmax(slot, m_old, l_old):
        st = st_scr[slot]
        m_new = jnp.maximum(m_old, jnp.max(st, axis=0, keepdims=True))
        alpha = jnp.exp2(m_old - m_new)
        p = jnp.exp2(st - m_new)
        p_scr[slot] = p.astype(p_scr.dtype)
        return m_new, alpha * l_old + jnp.sum(p, axis=0, keepdims=True), alpha

    def step(c, slot, carry, has_prev):
        m, l, alpha_prev = carry
        if has_prev:
            weighted_values(c - 1, 1 - slot, alpha_prev)
        if isinstance(c, int) and c == n_chunks - 1:
            stage_queries(qn_ref, cosn_ref, sinn_ref)
            scores(0, 0)
        else:
            scores(c + 1, 1 - slot)
        return softmax(slot, m, l)

    def pair(c, carry, first):
        carry = step(c, 0, carry, not first)
        return step(c + 1, 1, carry, True)

    @pl.when(pl.program_id(2) == 0)
    def _():
        stage_queries(q_ref, cos_ref, sin_ref)
        scores(0, 0)

    acc_scr[...] = jnp.zeros(acc_scr.shape, F32)
    rows = acc_scr.shape[1]
    zero = jnp.zeros((1, rows), F32)
    carry = (jnp.full((1, rows), -jnp.inf, F32), zero, zero)
    if n_pairs <= MAX_STATIC_PAIRS:
        for j in range(n_pairs):
            carry = pair(2 * j, carry, j == 0)
    else:
        carry = pair(0, carry, True)
        carry = lax.fori_loop(1, n_pairs - 1, lambda j, cr: pair(2 * j, cr, False), carry)
        carry = pair(n_chunks - 2, carry, False)
    _, l_fin, alpha_last = carry
    weighted_values(n_chunks - 1, 1, alpha_last)
    out = acc_scr[...] * (1.0 / l_fin)
    for g in range(GQA_GROUP):
        o_ref[:, g * HEAD_DIM:(g + 1) * HEAD_DIM] = out[:, g * tq:(g + 1) * tq].T.astype(o_ref.dtype)


def _attention(z3, q_norm, cosf, sinf, kr, vt):
    b, s, _ = kr.shape
    tq = _tile(s, 256)
    tk = _tile(s // 2, 512)
    rows = GQA_GROUP * tq
    gw = GQA_GROUP * HEAD_DIM
    n_tiles = s // tq

    def nxt(i):
        return jnp.minimum(i + 1, n_tiles - 1)

    return pl.pallas_call(
        functools.partial(_attn_kernel, tk=tk),
        grid=(b, N_KV_HEADS, n_tiles),
        in_specs=[pl.BlockSpec((None, tq, gw), lambda bi, h, i: (bi, i, h)),
                  pl.BlockSpec((None, tq, gw), lambda bi, h, i: (bi, nxt(i), h)),
                  pl.BlockSpec((tq, HEAD_DIM), lambda bi, h, i: (i, 0)),
                  pl.BlockSpec((tq, HEAD_DIM), lambda bi, h, i: (i, 0)),
                  pl.BlockSpec((tq, HEAD_DIM), lambda bi, h, i: (nxt(i), 0)),
                  pl.BlockSpec((tq, HEAD_DIM), lambda bi, h, i: (nxt(i), 0)),
                  pl.BlockSpec((1, HEAD_DIM), lambda bi, h, i: (0, 0)),
                  pl.BlockSpec((None, s, HEAD_DIM), lambda bi, h, i: (bi, 0, h)),
                  pl.BlockSpec((None, HEAD_DIM, s), lambda bi, h, i: (bi, h, 0))],
        out_specs=pl.BlockSpec((None, tq, gw), lambda bi, h, i: (bi, i, h)),
        out_shape=jax.ShapeDtypeStruct((b, s, ATTN_WIDTH), BF16),
        scratch_shapes=[pltpu.VMEM((HEAD_DIM, rows), BF16),
                        pltpu.VMEM((2, tk, rows), F32),
                        pltpu.VMEM((2, tk, rows), BF16),
                        pltpu.VMEM((HEAD_DIM, rows), F32)],
        compiler_params=_params(3),
        name="gqa_attention",
    )(z3, z3, cosf, sinf, cosf, sinf, q_norm.reshape(1, HEAD_DIM), kr, vt)


def _softplus(x):
    return jnp.maximum(x, 0.0) + jnp.log1p(jnp.exp(-jnp.abs(x)))


def _gelu_tanh(x):
    c = math.sqrt(2.0 / math.pi)
    return 0.5 * x * (1.0 + jnp.tanh(c * (x + 0.044715 * (x * x * x))))


def _lru_kernel(xr_ref, yr_ref, cw_ref, cb_ref, wg_ref, bg_ref, lam_ref, o_ref,
                xpad, hf, hb, *, chunk):
    s = xr_ref.shape[0]
    n_chunks = s // chunk
    n_tiles = chunk // 8
    w = LANES

    xpad[0:HALO, :] = jnp.zeros((HALO, w), F32)
    xpad[s + HALO:s + 2 * HALO, :] = jnp.zeros((HALO, w), F32)

    def fill(c, carry):
        t0 = pl.multiple_of(c * chunk, chunk)
        xpad[pl.ds(t0 + HALO, chunk), :] = xr_ref[pl.ds(t0, chunk), :].astype(F32)
        return carry

    lax.fori_loop(0, n_chunks, fill, 0)

    cw = cw_ref[...]
    cb = cb_ref[...]
    sp = _softplus(-lam_ref[...])
    sub = lax.broadcasted_iota(jnp.int32, (1, 8, w), 1)
    ext = chunk + 2 * HALO

    def conv(t0):
        out = cb
        for j in range(CONV_W):
            out = out + cw[j:j + 1] * xpad[pl.ds(t0 + (HALO - 1 + j), chunk), :]
        return out

    def gate_terms(xc, direction):
        col = 2 * w * direction
        g = jnp.dot(xc.astype(BF16), wg_ref[:, col:col + 2 * w],
                    preferred_element_type=F32) + bg_ref[:, col:col + 2 * w]
        r = jax.nn.sigmoid(g[:, :w])
        i = jax.nn.sigmoid(g[:, w:])
        log_a = (-LRU_C * r) * sp[direction:direction + 1]
        a = jnp.exp(log_a)
        u = jnp.sqrt(-jnp.tanh(log_a) * (a * a + 1.0)) * (i * xc)
        return a.reshape(n_tiles, 8, w), u.reshape(n_tiles, 8, w)

    def tile_scan(a, u, reverse):
        for d in (1, 2, 4):
            shift = (8 - d) if reverse else d
            keep = (sub < 8 - d) if reverse else (sub >= d)
            a_sh = pltpu.roll(a, shift, 1)
            u_sh = pltpu.roll(u, shift, 1)
            u = jnp.where(keep, u + a * u_sh, u)
            a = jnp.where(keep, a * a_sh, a)
        return a, u

    def body(c, carry):
        h_f, h_b = carry
        tf = pl.multiple_of(c * chunk, chunk)
        tb = pl.multiple_of((n_chunks - 1 - c) * chunk, chunk)
        a_f, u_f = tile_scan(*gate_terms(conv(tf), 0), reverse=False)
        a_b, u_b = tile_scan(*gate_terms(conv(tb), 1), reverse=True)
        for k in range(n_tiles):
            ht = u_f[k] + a_f[k] * h_f
            hf[pl.ds(tf + 8 * k, 8), :] = ht
            h_f = jnp.broadcast_to(ht[7:8, :], (8, w))
            kb = n_tiles - 1 - k
            ht = u_b[kb] + a_b[kb] * h_b
            hb[pl.ds(tb + 8 * kb, 8), :] = ht
            h_b = jnp.broadcast_to(ht[0:1, :], (8, w))
        return h_f, h_b

    zero = jnp.zeros((8, w), F32)
    lax.fori_loop(0, n_chunks, body, (zero, zero))

    def emit(c, carry):
        t0 = pl.multiple_of(c * chunk, chunk)
        y = yr_ref[pl.ds(t0, chunk), :].astype(F32)
        rec = (hf[pl.ds(t0, chunk), :] + hb[pl.ds(t0, chunk), :]) * _gelu_tanh(y)
        o_ref[pl.ds(t0, chunk), :] = rec.astype(o_ref.dtype)
        return carry

    lax.fori_loop(0, n_chunks, emit, 0)


def _lru(z3, conv_w, conv_b, w_a, b_a, w_i, b_i, lam, x_col0, y_col0):
    b, s, _ = z3.shape
    n_blocks = w_a.shape[1]
    d_rnn = n_blocks * LANES
    chunk = _tile(s, 256)
    wg = jnp.concatenate([w_a[0], w_i[0], w_a[1], w_i[1]], axis=-1).astype(BF16)
    bg = jnp.concatenate(
        [v.reshape(n_blocks, 1, LANES) for v in (b_a[0], b_i[0], b_a[1], b_i[1])], axis=-1)
    x_blk = x_col0 // LANES
    y_blk = y_col0 // LANES
    return pl.pallas_call(
        functools.partial(_lru_kernel, chunk=chunk),
        grid=(b, n_blocks),
        in_specs=[pl.BlockSpec((None, s, LANES), lambda bi, j: (bi, 0, x_blk + j)),
                  pl.BlockSpec((None, s, LANES), lambda bi, j: (bi, 0, y_blk + j)),
                  pl.BlockSpec((CONV_W, LANES), lambda bi, j: (0, j)),
                  pl.BlockSpec((1, LANES), lambda bi, j: (0, j)),
                  pl.BlockSpec((None, LANES, 4 * LANES), lambda bi, j: (j, 0, 0)),
                  pl.BlockSpec((None, 1, 4 * LANES), lambda bi, j: (j, 0, 0)),
                  pl.BlockSpec((2, LANES), lambda bi, j: (0, j))],
        out_specs=pl.BlockSpec((None, s, LANES), lambda bi, j: (bi, 0, j)),
        out_shape=jax.ShapeDtypeStruct((b, s, d_rnn), BF16),
        scratch_shapes=[pltpu.VMEM((s + 2 * HALO, LANES), F32),
                        pltpu.VMEM((s, LANES), F32),
                        pltpu.VMEM((s, LANES), F32)],
        compiler_params=_params(2),
        name="conv_rglru",
    )(z3, z3, conv_w, conv_b.reshape(1, d_rnn), wg, bg, lam)


def _encoder_layer(x2, b, s, p):
    d = x2.shape[1]
    d_rnn = p["w_rnn_out"].shape[0]
    c_x = ATTN_WIDTH
    c_y = c_x + d_rnn
    c_g = c_y + d_rnn

    h = _rmsnorm_cast(x2, p["norm_mix"])
    cosf, sinf = _rope_tables(s)
    kr, vt, z = _in_proj(h, p["w_in"], p["k_norm"], cosf, sinf, b, s)
    z3 = z.reshape(b, s, z.shape[1])
    attn = _attention(z3, p["q_norm"], cosf, sinf, kr, vt)
    rec = _lru(z3, p["conv_w"], p["conv_b"], p["lru_w_a"], p["lru_b_a"], p["lru_w_i"],
               p["lru_b_i"], p["lru_lambda"], c_x, c_y)

    merged = _merge(attn.reshape(b * s, ATTN_WIDTH), rec.reshape(b * s, d_rnn),
                    p["w_attn_out"], p["w_rnn_out"], z, p["b_gate"], c_g)
    x1 = _outproj(merged, p["w_out"], x2)
    hm = _rmsnorm_cast(x1, p["norm_mlp"])
    up = _mlp_up(hm, p["w_up"])
    return up, x1


def _trunk(x, layers, norm_final):
    b, s, d = x.shape
    x2 = x.reshape(b * s, d)
    for li, p in enumerate(layers):
        up, x1 = _encoder_layer(x2, b, s, p)
        x2 = _mlp_down(up, p["w_down"], x1, norm_final, final_norm=(li == len(layers) - 1))
    return x2.reshape(b, s, d)


def kernel(x_prompt, x_sample, norm_mix, w_in, q_norm, k_norm, conv_w, conv_b, lru_w_a, lru_b_a,
           lru_w_i, lru_b_i, lru_lambda, w_attn_out, w_rnn_out, b_gate, w_out, norm_mlp, w_up,
           w_down, norm_final):
    depth = w_in.shape[0]
    layers = []
    n_qk = ATTN_WIDTH + KV_WIDTH
    for l in range(depth):
        w_in_l = jnp.concatenate(
            [_split_pairs(w_in[l][:, :n_qk]), w_in[l][:, n_qk:]], axis=1).astype(BF16)
        layers.append(dict(
            norm_mix=norm_mix[l], w_in=w_in_l, q_norm=_split_pairs(q_norm[l]),
            k_norm=_split_pairs(k_norm[l]),
            conv_w=conv_w[l], conv_b=conv_b[l], lru_w_a=lru_w_a[l], lru_b_a=lru_b_a[l],
            lru_w_i=lru_w_i[l], lru_b_i=lru_b_i[l], lru_lambda=lru_lambda[l],
            w_attn_out=w_attn_out[l].astype(BF16), w_rnn_out=w_rnn_out[l].astype(BF16),
            b_gate=b_gate[l], w_out=w_out[l].astype(BF16), norm_mlp=norm_mlp[l],
            w_up=w_up[l].astype(BF16), w_down=w_down[l].astype(BF16)))
    y_prompt = _trunk(x_prompt, layers, norm_final)
    y_sample = _trunk(x_sample, layers, norm_final)
    return (y_prompt, y_sample)
```

```python
import functools
import math

import jax
import jax.numpy as jnp
from jax import lax
from jax.experimental import pallas as pl
from jax.experimental.pallas import tpu as pltpu

F32 = jnp.float32
BF16 = jnp.bfloat16

HEAD_DIM = 128
N_Q_HEADS = 16
N_KV_HEADS = 4
GQA_GROUP = N_Q_HEADS // N_KV_HEADS
ATTN_WIDTH = N_Q_HEADS * HEAD_DIM
KV_WIDTH = N_KV_HEADS * HEAD_DIM
ROPE_THETA = 10000.0
GRID_W = 64
LRU_C = 8.0
CONV_W = 4
NORM_EPS = 1e-6
F32_TINY = 1.1754944e-38
LANES = 128
MXU_COLS = 256
HALO = 16
V7X_VMEM_LIMIT = 60 * 1024 * 1024


def _params(n_axes, vmem=V7X_VMEM_LIMIT):
    return pltpu.CompilerParams(
        dimension_semantics=("arbitrary",) * n_axes, vmem_limit_bytes=vmem)


def _tile(n, want):
    t = min(n, want)
    while n % t:
        t //= 2
    return t


def _rmsnorm_cast_kernel(x_ref, g_ref, o_ref):
    x = x_ref[...]
    ms = jnp.mean(x * x, axis=-1, keepdims=True)
    o_ref[...] = (x * lax.rsqrt(ms + NORM_EPS) * g_ref[...]).astype(o_ref.dtype)


def _rmsnorm_cast(x, g):
    t, d = x.shape
    tr = _tile(t, 256)
    return pl.pallas_call(
        _rmsnorm_cast_kernel,
        grid=(t // tr,),
        in_specs=[pl.BlockSpec((tr, d), lambda i: (i, 0)),
                  pl.BlockSpec((1, d), lambda i: (0, 0))],
        out_specs=pl.BlockSpec((tr, d), lambda i: (i, 0)),
        out_shape=jax.ShapeDtypeStruct((t, d), BF16),
        compiler_params=_params(1),
        name="rmsnorm_cast",
    )(x, g.reshape(1, d))


def _mlp_up_kernel(a_ref, w_ref, o_ref):
    y = jnp.dot(a_ref[...], w_ref[...], preferred_element_type=F32)
    r = jnp.maximum(y, 0.0)
    o_ref[...] = (r * r).astype(o_ref.dtype)


def _mlp_up(a, w):
    m, k = a.shape
    n = w.shape[1]
    tm, tn = _tile(m, 1024), _tile(n, 1024)
    return pl.pallas_call(
        _mlp_up_kernel,
        grid=(m // tm, n // tn),
        in_specs=[pl.BlockSpec((tm, k), lambda i, j: (i, 0)),
                  pl.BlockSpec((k, tn), lambda i, j: (0, j))],
        out_specs=pl.BlockSpec((tm, tn), lambda i, j: (i, j)),
        out_shape=jax.ShapeDtypeStruct((m, n), BF16),
        compiler_params=_params(2),
        name="mlp_up",
    )(a, w)


def _merge_kernel(attn_ref, rec_ref, wa_ref, wr_ref, ga_ref, gr_ref, ba_ref, br_ref, o_ref):
    ab = jnp.dot(attn_ref[...], wa_ref[...], preferred_element_type=F32)
    rb = jnp.dot(rec_ref[...], wr_ref[...], preferred_element_type=F32)
    ga = jax.nn.sigmoid(ga_ref[...].astype(F32) + ba_ref[...])
    gr = jax.nn.sigmoid(gr_ref[...].astype(F32) + br_ref[...])
    o_ref[...] = (ga * ab + gr * rb).astype(o_ref.dtype)


def _merge(attn, rec, wa, wr, z, b_gate, gate_col0):
    m, ka = attn.shape
    kr = rec.shape[1]
    n = wa.shape[1]
    tm, tn = _tile(m, 1024), _tile(n, 512)
    ga_blk = gate_col0 // tn
    gr_blk = (gate_col0 + n) // tn
    return pl.pallas_call(
        _merge_kernel,
        grid=(m // tm, n // tn),
        in_specs=[pl.BlockSpec((tm, ka), lambda i, j: (i, 0)),
                  pl.BlockSpec((tm, kr), lambda i, j: (i, 0)),
                  pl.BlockSpec((ka, tn), lambda i, j: (0, j)),
                  pl.BlockSpec((kr, tn), lambda i, j: (0, j)),
                  pl.BlockSpec((tm, tn), lambda i, j: (i, ga_blk + j)),
                  pl.BlockSpec((tm, tn), lambda i, j: (i, gr_blk + j)),
                  pl.BlockSpec((1, tn), lambda i, j: (0, j)),
                  pl.BlockSpec((1, tn), lambda i, j: (0, j))],
        out_specs=pl.BlockSpec((tm, tn), lambda i, j: (i, j)),
        out_shape=jax.ShapeDtypeStruct((m, n), BF16),
        compiler_params=_params(2),
        name="branch_merge",
    )(attn, rec, wa, wr, z, z, b_gate[0:1], b_gate[1:2])


def _outproj_kernel(a_ref, w_ref, x_ref, o_ref):
    o_ref[...] = x_ref[...] + jnp.dot(a_ref[...], w_ref[...], preferred_element_type=F32)


def _outproj(a, w, x):
    m, k = a.shape
    n = w.shape[1]
    tm, tn = _tile(m, 1024), _tile(n, 512)
    return pl.pallas_call(
        _outproj_kernel,
        grid=(m // tm, n // tn),
        in_specs=[pl.BlockSpec((tm, k), lambda i, j: (i, 0)),
                  pl.BlockSpec((k, tn), lambda i, j: (0, j)),
                  pl.BlockSpec((tm, tn), lambda i, j: (i, j))],
        out_specs=pl.BlockSpec((tm, tn), lambda i, j: (i, j)),
        out_shape=jax.ShapeDtypeStruct((m, n), F32),
        compiler_params=_params(2),
        name="out_proj",
    )(a, w, x)


def _mlp_down_kernel(u_ref, w_ref, x_ref, g_ref, o_ref, *, final_norm):
    kk = pl.program_id(1)

    @pl.when(kk == 0)
    def _():
        o_ref[...] = x_ref[...]

    o_ref[...] += jnp.dot(u_ref[...], w_ref[...], preferred_element_type=F32)

    if final_norm:
        @pl.when(kk == pl.num_programs(1) - 1)
        def _():
            x = o_ref[...]
            ms = jnp.mean(x * x, axis=-1, keepdims=True)
            o_ref[...] = x * lax.rsqrt(ms + NORM_EPS) * g_ref[...]


def _mlp_down(u, w, x, g, final_norm):
    m, k = u.shape
    n = w.shape[1]
    tm, tk = _tile(m, 512), _tile(k, 1024)
    return pl.pallas_call(
        functools.partial(_mlp_down_kernel, final_norm=final_norm),
        grid=(m // tm, k // tk),
        in_specs=[pl.BlockSpec((tm, tk), lambda i, kk: (i, kk)),
                  pl.BlockSpec((tk, n), lambda i, kk: (kk, 0)),
                  pl.BlockSpec((tm, n), lambda i, kk: (i, 0), pipeline_mode=pl.Buffered(1)),
                  pl.BlockSpec((1, n), lambda i, kk: (0, 0))],
        out_specs=pl.BlockSpec((tm, n), lambda i, kk: (i, 0)),
        out_shape=jax.ShapeDtypeStruct((m, n), F32),
        compiler_params=_params(2),
        name="mlp_down_norm",
    )(u, w, x, g.reshape(1, n))


Q_SCALE = math.log2(math.e) * HEAD_DIM ** -0.5


def _rope_tables(seq_len):
    axis_dim = HEAD_DIM // 2
    n_rows = seq_len // GRID_W
    row = jnp.repeat(jnp.arange(n_rows), GRID_W).astype(F32)
    col = jnp.tile(jnp.arange(GRID_W), n_rows).astype(F32)
    inv = ROPE_THETA ** (-jnp.arange(0, axis_dim, 2, dtype=F32) / axis_dim)
    ang = jnp.concatenate([row[:, None] * inv, col[:, None] * inv], axis=-1)
    cos, sin = jnp.cos(ang), jnp.sin(ang)
    cosf = jnp.repeat(cos, 2, axis=-1)
    sinf = jnp.stack([-sin, sin], axis=-1).reshape(seq_len, HEAD_DIM)
    return cosf, sinf


IN_PROJ_TN = 2 * KV_WIDTH


def _norm_rope(x, g, cos, sin):
    ms = jnp.mean(x * x, axis=-1, keepdims=True)
    xn = x * lax.rsqrt(ms + NORM_EPS) * g
    even = (lax.broadcasted_iota(jnp.int32, xn.shape, 1) & 1) == 0
    nxt = pltpu.roll(xn, HEAD_DIM - 1, 1)
    prv = pltpu.roll(xn, 1, 1)
    return xn * cos + jnp.where(even, nxt, prv) * sin


def _in_proj_kernel(a_ref, w_ref, cos_ref, sin_ref, kg_ref, ko_ref, vt_ref, z_ref, *, kv_tile):
    j = pl.program_id(1)

    def product():
        return jnp.dot(a_ref[...], w_ref[...], preferred_element_type=F32)

    @pl.when(j == kv_tile)
    def _():
        y = product()
        for h in range(N_KV_HEADS):
            sl = slice(h * HEAD_DIM, (h + 1) * HEAD_DIM)
            ko_ref[:, sl] = _norm_rope(y[:, sl], kg_ref[...], cos_ref[...],
                                       sin_ref[...]).astype(ko_ref.dtype)
            vsl = slice(KV_WIDTH + h * HEAD_DIM, KV_WIDTH + (h + 1) * HEAD_DIM)
            vt_ref[sl, :] = y[:, vsl].T.astype(vt_ref.dtype)

    @pl.when(j != kv_tile)
    def _():
        z_ref[...] = product().astype(z_ref.dtype)


def _in_proj(h, w, k_norm, cosf, sinf, b, s):
    t, k = h.shape
    n = w.shape[1]
    tn = IN_PROJ_TN
    tm = _tile(s, 1024)
    assert ATTN_WIDTH % tn == 0 and n % tn == 0
    kv_tile = ATTN_WIDTH // tn
    tps = s // tm

    def z_block(i, j):
        return i, jnp.where(j < kv_tile, j, j - 1)

    return pl.pallas_call(
        functools.partial(_in_proj_kernel, kv_tile=kv_tile),
        grid=(t // tm, n // tn),
        in_specs=[pl.BlockSpec((tm, k), lambda i, j: (i, 0)),
                  pl.BlockSpec((k, tn), lambda i, j: (0, j)),
                  pl.BlockSpec((tm, HEAD_DIM), lambda i, j: (i % tps, 0)),
                  pl.BlockSpec((tm, HEAD_DIM), lambda i, j: (i % tps, 0)),
                  pl.BlockSpec((1, HEAD_DIM), lambda i, j: (0, 0))],
        out_specs=[pl.BlockSpec((None, tm, KV_WIDTH), lambda i, j: (i // tps, i % tps, 0)),
                   pl.BlockSpec((None, KV_WIDTH, tm), lambda i, j: (i // tps, 0, i % tps)),
                   pl.BlockSpec((tm, tn), z_block)],
        out_shape=[jax.ShapeDtypeStruct((b, s, KV_WIDTH), BF16),
                   jax.ShapeDtypeStruct((b, KV_WIDTH, s), BF16),
                   jax.ShapeDtypeStruct((t, n - tn), BF16)],
        compiler_params=_params(2),
        name="in_proj",
    )(h, w, cosf, sinf, k_norm.reshape(1, HEAD_DIM))


MAX_STATIC_PAIRS = 8


def _attn_kernel(q_ref, qn_ref, cos_ref, sin_ref, cosn_ref, sinn_ref, qg_ref, k_ref, vt_ref, o_ref,
                 qt_scr, st_scr, p_scr, acc_scr, *, tk):
    tq = q_ref.shape[0]
    n_pairs = k_ref.shape[0] // (2 * tk)
    n_chunks = 2 * n_pairs

    def stage_queries(src_ref, c_ref, s_ref):
        for g in range(GQA_GROUP):
            x = src_ref[:, g * HEAD_DIM:(g + 1) * HEAD_DIM].astype(F32)
            q = _norm_rope(x, qg_ref[...], c_ref[...], s_ref[...]) * Q_SCALE
            qt_scr[:, g * tq:(g + 1) * tq] = q.T.astype(qt_scr.dtype)

    def scores(c, slot):
        start = pl.multiple_of(c * tk, tk)
        st_scr[slot] = jnp.dot(k_ref[pl.ds(start, tk), :], qt_scr[...],
                               preferred_element_type=F32)

    def weighted_values(c, slot, alpha):
        start = pl.multiple_of(c * tk, tk)
        acc_scr[...] = alpha * acc_scr[...] + jnp.dot(
            vt_ref[:, pl.ds(start, tk)], p_scr[slot], preferred_element_type=F32)

    def softmax(slot, m_old, l_old):
        st = st_scr[slot]
        m_new = jnp.maximum(m_old, jnp.max(st, axis=0, keepdims=True))
        alpha = jnp.exp2(m_old - m_new)
        p = jnp.exp2(st - m_new)
        p_scr[slot] = p.astype(p_scr.dtype)
        return m_new, alpha * l_old + jnp.sum(p, axis=0, keepdims=True), alpha

    def step(c, slot, carry, has_prev):
        m, l, alpha_prev = carry
        if has_prev:
            weighted_values(c - 1, 1 - slot, alpha_prev)
        if isinstance(c, int) and c == n_chunks - 1:
            stage_queries(qn_ref, cosn_ref, sinn_ref)
            scores(0, 0)
        else:
            scores(c + 1, 1 - slot)
        return softmax(slot, m, l)

    def pair(c, carry, first):
        carry = step(c, 0, carry, not first)
        return step(c + 1, 1, carry, True)

    @pl.when(pl.program_id(2) == 0)
    def _():
        stage_queries(q_ref, cos_ref, sin_ref)
        scores(0, 0)

    acc_scr[...] = jnp.zeros(acc_scr.shape, F32)
    rows = acc_scr.shape[1]
    zero = jnp.zeros((1, rows), F32)
    carry = (jnp.full((1, rows), -jnp.inf, F32), zero, zero)
    if n_pairs <= MAX_STATIC_PAIRS:
        for j in range(n_pairs):
            carry = pair(2 * j, carry, j == 0)
    else:
        carry = pair(0, carry, True)
        carry = lax.fori_loop(1, n_pairs - 1, lambda j, cr: pair(2 * j, cr, False), carry)
        carry = pair(n_chunks - 2, carry, False)
    _, l_fin, alpha_last = carry
    weighted_values(n_chunks - 1, 1, alpha_last)
    out = acc_scr[...] * (1.0 / l_fin)
    for g in range(GQA_GROUP):
        o_ref[:, g * HEAD_DIM:(g + 1) * HEAD_DIM] = out[:, g * tq:(g + 1) * tq].T.astype(o_ref.dtype)


def _attention(z3, q_norm, cosf, sinf, kr, vt):
    b, s, _ = kr.shape
    tq = _tile(s, 256)
    tk = _tile(s // 2, 512)
    rows = GQA_GROUP * tq
    gw = GQA_GROUP * HEAD_DIM
    n_tiles = s // tq

    def nxt(i):
        return jnp.minimum(i + 1, n_tiles - 1)

    return pl.pallas_call(
        functools.partial(_attn_kernel, tk=tk),
        grid=(b, N_KV_HEADS, n_tiles),
        in_specs=[pl.BlockSpec((None, tq, gw), lambda bi, h, i: (bi, i, h)),
                  pl.BlockSpec((None, tq, gw), lambda bi, h, i: (bi, nxt(i), h)),
                  pl.BlockSpec((tq, HEAD_DIM), lambda bi, h, i: (i, 0)),
                  pl.BlockSpec((tq, HEAD_DIM), lambda bi, h, i: (i, 0)),
                  pl.BlockSpec((tq, HEAD_DIM), lambda bi, h, i: (nxt(i), 0)),
                  pl.BlockSpec((tq, HEAD_DIM), lambda bi, h, i: (nxt(i), 0)),
                  pl.BlockSpec((1, HEAD_DIM), lambda bi, h, i: (0, 0)),
                  pl.BlockSpec((None, s, HEAD_DIM), lambda bi, h, i: (bi, 0, h)),
                  pl.BlockSpec((None, HEAD_DIM, s), lambda bi, h, i: (bi, h, 0))],
        out_specs=pl.BlockSpec((None, tq, gw), lambda bi, h, i: (bi, i, h)),
        out_shape=jax.ShapeDtypeStruct((b, s, ATTN_WIDTH), BF16),
        scratch_shapes=[pltpu.VMEM((HEAD_DIM, rows), BF16),
                        pltpu.VMEM((2, tk, rows), F32),
                        pltpu.VMEM((2, tk, rows), BF16),
                        pltpu.VMEM((HEAD_DIM, rows), F32)],
        compiler_params=_params(3),
        name="gqa_attention",
    )(z3, z3, cosf, sinf, cosf, sinf, q_norm.reshape(1, HEAD_DIM), kr, vt)


def _softplus(x):
    return jnp.maximum(x, 0.0) + jnp.log1p(jnp.exp(-jnp.abs(x)))


def _gelu_tanh(x):
    c = math.sqrt(2.0 / math.pi)
    return 0.5 * x * (1.0 + jnp.tanh(c * (x + 0.044715 * (x * x * x))))


def _lru_kernel(xr_ref, yr_ref, cw_ref, cb_ref, wg_ref, bg_ref, lam_ref, o_ref,
                xpad, hf, hb, *, chunk):
    s = xr_ref.shape[0]
    n_chunks = s // chunk
    n_tiles = chunk // 8
    w = LANES

    xpad[0:HALO, :] = jnp.zeros((HALO, w), F32)
    xpad[s + HALO:s + 2 * HALO, :] = jnp.zeros((HALO, w), F32)

    def fill(c, carry):
        t0 = pl.multiple_of(c * chunk, chunk)
        xpad[pl.ds(t0 + HALO, chunk), :] = xr_ref[pl.ds(t0, chunk), :].astype(F32)
        return carry

    lax.fori_loop(0, n_chunks, fill, 0)

    cw = cw_ref[...]
    cb = cb_ref[...]
    neg_c_sp = -LRU_C * _softplus(-lam_ref[...])
    sub = lax.broadcasted_iota(jnp.int32, (1, 8, w), 1)
    ext = chunk + 2 * HALO

    def conv(t0):
        out = cb
        for j in range(CONV_W):
            out = out + cw[j:j + 1] * xpad[pl.ds(t0 + (HALO - 1 + j), chunk), :]
        return out

    def gate_terms(xc, direction):
        col = 2 * w * direction
        g = jnp.dot(xc.astype(BF16), wg_ref[:, col:col + 2 * w],
                    preferred_element_type=F32) + bg_ref[:, col:col + 2 * w]
        r = jax.nn.sigmoid(g[:, :w])
        i = jax.nn.sigmoid(g[:, w:])
        log_a = r * neg_c_sp[direction:direction + 1]
        a = jnp.exp(log_a)
        m = -jnp.tanh(log_a) * (a * a + 1.0)
        root = m * lax.rsqrt(jnp.maximum(m, F32_TINY))
        u = root * (i * xc)
        return a.reshape(n_tiles, 8, w), u.reshape(n_tiles, 8, w)

    def tile_scan(a, u, reverse):
        for d in (1, 2, 4):
            shift = (8 - d) if reverse else d
            keep = (sub < 8 - d) if reverse else (sub >= d)
            a_sh = pltpu.roll(a, shift, 1)
            u_sh = pltpu.roll(u, shift, 1)
            u = jnp.where(keep, u + a * u_sh, u)
            a = jnp.where(keep, a * a_sh, a)
        return a, u

    def body(c, carry):
        h_f, h_b = carry
        tf = pl.multiple_of(c * chunk, chunk)
        tb = pl.multiple_of((n_chunks - 1 - c) * chunk, chunk)
        a_f, u_f = tile_scan(*gate_terms(conv(tf), 0), reverse=False)
        a_b, u_b = tile_scan(*gate_terms(conv(tb), 1), reverse=True)
        for k in range(n_tiles):
            ht = u_f[k] + a_f[k] * h_f
            hf[pl.ds(tf + 8 * k, 8), :] = ht
            h_f = jnp.broadcast_to(ht[7:8, :], (8, w))
            kb = n_tiles - 1 - k
            ht = u_b[kb] + a_b[kb] * h_b
            hb[pl.ds(tb + 8 * kb, 8), :] = ht
            h_b = jnp.broadcast_to(ht[0:1, :], (8, w))
        return h_f, h_b

    zero = jnp.zeros((8, w), F32)
    lax.fori_loop(0, n_chunks, body, (zero, zero))

    def emit(c, carry):
        t0 = pl.multiple_of(c * chunk, chunk)
        y = yr_ref[pl.ds(t0, chunk), :].astype(F32)
        rec = (hf[pl.ds(t0, chunk), :] + hb[pl.ds(t0, chunk), :]) * _gelu_tanh(y)
        o_ref[pl.ds(t0, chunk), :] = rec.astype(o_ref.dtype)
        return carry

    lax.fori_loop(0, n_chunks, emit, 0)


def _lru(z3, conv_w, conv_b, w_a, b_a, w_i, b_i, lam, x_col0, y_col0):
    b, s, _ = z3.shape
    n_blocks = w_a.shape[1]
    d_rnn = n_blocks * LANES
    chunk = _tile(s, 256)
    wg = jnp.concatenate([w_a[0], w_i[0], w_a[1], w_i[1]], axis=-1).astype(BF16)
    bg = jnp.concatenate(
        [v.reshape(n_blocks, 1, LANES) for v in (b_a[0], b_i[0], b_a[1], b_i[1])], axis=-1)
    x_blk = x_col0 // LANES
    y_blk = y_col0 // LANES
    return pl.pallas_call(
        functools.partial(_lru_kernel, chunk=chunk),
        grid=(b, n_blocks),
        in_specs=[pl.BlockSpec((None, s, LANES), lambda bi, j: (bi, 0, x_blk + j)),
                  pl.BlockSpec((None, s, LANES), lambda bi, j: (bi, 0, y_blk + j)),
                  pl.BlockSpec((CONV_W, LANES), lambda bi, j: (0, j)),
                  pl.BlockSpec((1, LANES), lambda bi, j: (0, j)),
                  pl.BlockSpec((None, LANES, 4 * LANES), lambda bi, j: (j, 0, 0)),
                  pl.BlockSpec((None, 1, 4 * LANES), lambda bi, j: (j, 0, 0)),
                  pl.BlockSpec((2, LANES), lambda bi, j: (0, j))],
        out_specs=pl.BlockSpec((None, s, LANES), lambda bi, j: (bi, 0, j)),
        out_shape=jax.ShapeDtypeStruct((b, s, d_rnn), BF16),
        scratch_shapes=[pltpu.VMEM((s + 2 * HALO, LANES), F32),
                        pltpu.VMEM((s, LANES), F32),
                        pltpu.VMEM((s, LANES), F32)],
        compiler_params=_params(2),
        name="conv_rglru",
    )(z3, z3, conv_w, conv_b.reshape(1, d_rnn), wg, bg, lam)


def _encoder_layer(x2, b, s, p):
    d = x2.shape[1]
    d_rnn = p["w_rnn_out"].shape[0]
    c_x = ATTN_WIDTH
    c_y = c_x + d_rnn
    c_g = c_y + d_rnn

    h = _rmsnorm_cast(x2, p["norm_mix"])
    cosf, sinf = _rope_tables(s)
    kr, vt, z = _in_proj(h, p["w_in"], p["k_norm"], cosf, sinf, b, s)
    z3 = z.reshape(b, s, z.shape[1])
    attn = _attention(z3, p["q_norm"], cosf, sinf, kr, vt)
    rec = _lru(z3, p["conv_w"], p["conv_b"], p["lru_w_a"], p["lru_b_a"], p["lru_w_i"],
               p["lru_b_i"], p["lru_lambda"], c_x, c_y)

    merged = _merge(attn.reshape(b * s, ATTN_WIDTH), rec.reshape(b * s, d_rnn),
                    p["w_attn_out"], p["w_rnn_out"], z, p["b_gate"], c_g)
    x1 = _outproj(merged, p["w_out"], x2)
    hm = _rmsnorm_cast(x1, p["norm_mlp"])
    up = _mlp_up(hm, p["w_up"])
    return up, x1


def _trunk(x, layers, norm_final):
    b, s, d = x.shape
    x2 = x.reshape(b * s, d)
    for li, p in enumerate(layers):
        up, x1 = _encoder_layer(x2, b, s, p)
        x2 = _mlp_down(up, p["w_down"], x1, norm_final, final_norm=(li == len(layers) - 1))
    return x2.reshape(b, s, d)


def kernel(x_prompt, x_sample, norm_mix, w_in, q_norm, k_norm, conv_w, conv_b, lru_w_a, lru_b_a,
           lru_w_i, lru_b_i, lru_lambda, w_attn_out, w_rnn_out, b_gate, w_out, norm_mlp, w_up,
           w_down, norm_final):
    depth = w_in.shape[0]
    layers = []
    for l in range(depth):
        layers.append(dict(
            norm_mix=norm_mix[l], w_in=w_in[l].astype(BF16), q_norm=q_norm[l], k_norm=k_norm[l],
            conv_w=conv_w[l], conv_b=conv_b[l], lru_w_a=lru_w_a[l], lru_b_a=lru_b_a[l],
            lru_w_i=lru_w_i[l], lru_b_i=lru_b_i[l], lru_lambda=lru_lambda[l],
            w_attn_out=w_attn_out[l].astype(BF16), w_rnn_out=w_rnn_out[l].astype(BF16),
            b_gate=b_gate[l], w_out=w_out[l].astype(BF16), norm_mlp=norm_mlp[l],
            w_up=w_up[l].astype(BF16), w_down=w_down[l].astype(BF16)))
    y_prompt = _trunk(x_prompt, layers, norm_final)
    y_sample = _trunk(x_sample, layers, norm_final)
    return (y_prompt, y_sample)
```

```python
import functools
import math

import jax
import jax.numpy as jnp
from jax import lax
from jax.experimental import pallas as pl
from jax.experimental.pallas import tpu as pltpu

F32 = jnp.float32
BF16 = jnp.bfloat16

HEAD_DIM = 128
N_Q_HEADS = 16
N_KV_HEADS = 4
GQA_GROUP = N_Q_HEADS // N_KV_HEADS
ATTN_WIDTH = N_Q_HEADS * HEAD_DIM
KV_WIDTH = N_KV_HEADS * HEAD_DIM
ROPE_THETA = 10000.0
GRID_W = 64
LRU_C = 8.0
CONV_W = 4
NORM_EPS = 1e-6
F32_TINY = 1.1754944e-38
LANES = 128
MXU_COLS = 256
HALO = 16
V7X_VMEM_LIMIT = 60 * 1024 * 1024


def _params(n_axes, vmem=V7X_VMEM_LIMIT):
    return pltpu.CompilerParams(
        dimension_semantics=("arbitrary",) * n_axes, vmem_limit_bytes=vmem)


def _tile(n, want):
    t = min(n, want)
    while n % t:
        t //= 2
    return t


def _rmsnorm_cast_kernel(x_ref, g_ref, o_ref):
    x = x_ref[...]
    ms = jnp.mean(x * x, axis=-1, keepdims=True)
    o_ref[...] = (x * lax.rsqrt(ms + NORM_EPS) * g_ref[...]).astype(o_ref.dtype)


def _rmsnorm_cast(x, g, out_dtype=BF16):
    t, d = x.shape
    tr = _tile(t, 256)
    return pl.pallas_call(
        _rmsnorm_cast_kernel,
        grid=(t // tr,),
        in_specs=[pl.BlockSpec((tr, d), lambda i: (i, 0)),
                  pl.BlockSpec((1, d), lambda i: (0, 0))],
        out_specs=pl.BlockSpec((tr, d), lambda i: (i, 0)),
        out_shape=jax.ShapeDtypeStruct((t, d), out_dtype),
        compiler_params=_params(1),
        name="rmsnorm_cast",
    )(x, g.reshape(1, d))


def _mlp_up_kernel(a_ref, w_ref, o_ref):
    y = jnp.dot(a_ref[...], w_ref[...], preferred_element_type=F32)
    r = jnp.maximum(y, 0.0)
    o_ref[...] = (r * r).astype(o_ref.dtype)


def _mlp_up(a, w):
    m, k = a.shape
    n = w.shape[1]
    tm, tn = _tile(m, 1024), _tile(n, 1024)
    return pl.pallas_call(
        _mlp_up_kernel,
        grid=(m // tm, n // tn),
        in_specs=[pl.BlockSpec((tm, k), lambda i, j: (i, 0)),
                  pl.BlockSpec((k, tn), lambda i, j: (0, j))],
        out_specs=pl.BlockSpec((tm, tn), lambda i, j: (i, j)),
        out_shape=jax.ShapeDtypeStruct((m, n), BF16),
        compiler_params=_params(2),
        name="mlp_up",
    )(a, w)


def _merge_kernel(attn_ref, rec_ref, wa_ref, wr_ref, ga_ref, gr_ref, ba_ref, br_ref, o_ref):
    ab = jnp.dot(attn_ref[...], wa_ref[...], preferred_element_type=F32)
    rb = jnp.dot(rec_ref[...], wr_ref[...], preferred_element_type=F32)
    ga = jax.nn.sigmoid(ga_ref[...].astype(F32) + ba_ref[...])
    gr = jax.nn.sigmoid(gr_ref[...].astype(F32) + br_ref[...])
    o_ref[...] = (ga * ab + gr * rb).astype(o_ref.dtype)


def _merge(attn, rec, wa, wr, z, b_gate, gate_col0):
    m, ka = attn.shape
    kr = rec.shape[1]
    n = wa.shape[1]
    tm, tn = _tile(m, 1024), _tile(n, 1024)
    ga_blk = gate_col0 // tn
    gr_blk = (gate_col0 + n) // tn
    return pl.pallas_call(
        _merge_kernel,
        grid=(m // tm, n // tn),
        in_specs=[pl.BlockSpec((tm, ka), lambda i, j: (i, 0)),
                  pl.BlockSpec((tm, kr), lambda i, j: (i, 0)),
                  pl.BlockSpec((ka, tn), lambda i, j: (0, j)),
                  pl.BlockSpec((kr, tn), lambda i, j: (0, j)),
                  pl.BlockSpec((tm, tn), lambda i, j: (i, ga_blk + j)),
                  pl.BlockSpec((tm, tn), lambda i, j: (i, gr_blk + j)),
                  pl.BlockSpec((1, tn), lambda i, j: (0, j)),
                  pl.BlockSpec((1, tn), lambda i, j: (0, j))],
        out_specs=pl.BlockSpec((tm, tn), lambda i, j: (i, j)),
        out_shape=jax.ShapeDtypeStruct((m, n), BF16),
        compiler_params=_params(2),
        name="branch_merge",
    )(attn, rec, wa, wr, z, z, b_gate[0:1], b_gate[1:2])


def _outproj_kernel(a_ref, w_ref, x_ref, o_ref):
    o_ref[...] = x_ref[...] + jnp.dot(a_ref[...], w_ref[...], preferred_element_type=F32)


def _outproj(a, w, x):
    m, k = a.shape
    n = w.shape[1]
    tm, tn = _tile(m, 1024), _tile(n, 1024)
    return pl.pallas_call(
        _outproj_kernel,
        grid=(m // tm, n // tn),
        in_specs=[pl.BlockSpec((tm, k), lambda i, j: (i, 0)),
                  pl.BlockSpec((k, tn), lambda i, j: (0, j)),
                  pl.BlockSpec((tm, tn), lambda i, j: (i, j))],
        out_specs=pl.BlockSpec((tm, tn), lambda i, j: (i, j)),
        out_shape=jax.ShapeDtypeStruct((m, n), F32),
        compiler_params=_params(2),
        name="out_proj",
    )(a, w, x)


def _mlp_down_kernel(u_ref, w_ref, x_ref, o_ref):
    def product():
        return jnp.dot(u_ref[...], w_ref[...], preferred_element_type=F32)

    @pl.when(pl.program_id(2) == 0)
    def _():
        o_ref[...] = x_ref[...] + product()

    @pl.when(pl.program_id(2) != 0)
    def _():
        o_ref[...] += product()


def _mlp_down(u, w, x):
    m, k = u.shape
    n = w.shape[1]
    tm, tn, tk = _tile(m, 1024), _tile(n, 1024), _tile(k, 4096)
    return pl.pallas_call(
        _mlp_down_kernel,
        grid=(m // tm, n // tn, k // tk),
        in_specs=[pl.BlockSpec((tm, tk), lambda i, j, kk: (i, kk)),
                  pl.BlockSpec((tk, tn), lambda i, j, kk: (kk, j)),
                  pl.BlockSpec((tm, tn), lambda i, j, kk: (i, j))],
        out_specs=pl.BlockSpec((tm, tn), lambda i, j, kk: (i, j)),
        out_shape=jax.ShapeDtypeStruct((m, n), F32),
        compiler_params=_params(3),
        name="mlp_down",
    )(u, w, x)


Q_SCALE = math.log2(math.e) * HEAD_DIM ** -0.5


def _rope_tables(seq_len):
    axis_dim = HEAD_DIM // 2
    n_rows = seq_len // GRID_W
    row = jnp.repeat(jnp.arange(n_rows), GRID_W).astype(F32)
    col = jnp.tile(jnp.arange(GRID_W), n_rows).astype(F32)
    inv = ROPE_THETA ** (-jnp.arange(0, axis_dim, 2, dtype=F32) / axis_dim)
    ang = jnp.concatenate([row[:, None] * inv, col[:, None] * inv], axis=-1)
    cos, sin = jnp.cos(ang), jnp.sin(ang)
    cosf = jnp.repeat(cos, 2, axis=-1)
    sinf = jnp.stack([-sin, sin], axis=-1).reshape(seq_len, HEAD_DIM)
    return cosf, sinf


IN_PROJ_TN = 2 * KV_WIDTH


def _norm_rope(x, g, cos, sin):
    ms = jnp.mean(x * x, axis=-1, keepdims=True)
    xn = x * lax.rsqrt(ms + NORM_EPS) * g
    even = (lax.broadcasted_iota(jnp.int32, xn.shape, 1) & 1) == 0
    nxt = pltpu.roll(xn, HEAD_DIM - 1, 1)
    prv = pltpu.roll(xn, 1, 1)
    return xn * cos + jnp.where(even, nxt, prv) * sin


def _in_proj_kernel(a_ref, w_ref, cos_ref, sin_ref, kg_ref, ko_ref, vt_ref, z_ref, *, kv_tile):
    j = pl.program_id(1)

    def product():
        return jnp.dot(a_ref[...], w_ref[...], preferred_element_type=F32)

    @pl.when(j == kv_tile)
    def _():
        y = product()
        for h in range(N_KV_HEADS):
            sl = slice(h * HEAD_DIM, (h + 1) * HEAD_DIM)
            ko_ref[:, sl] = _norm_rope(y[:, sl], kg_ref[...], cos_ref[...],
                                       sin_ref[...]).astype(ko_ref.dtype)
            vsl = slice(KV_WIDTH + h * HEAD_DIM, KV_WIDTH + (h + 1) * HEAD_DIM)
            vt_ref[sl, :] = y[:, vsl].T.astype(vt_ref.dtype)

    @pl.when(j != kv_tile)
    def _():
        z_ref[...] = product().astype(z_ref.dtype)


def _in_proj(h, w, k_norm, cosf, sinf, b, s):
    t, k = h.shape
    n = w.shape[1]
    tn = IN_PROJ_TN
    tm = _tile(s, 1024)
    assert ATTN_WIDTH % tn == 0 and n % tn == 0
    kv_tile = ATTN_WIDTH // tn
    tps = s // tm

    def z_block(i, j):
        return i, jnp.where(j < kv_tile, j, j - 1)

    return pl.pallas_call(
        functools.partial(_in_proj_kernel, kv_tile=kv_tile),
        grid=(t // tm, n // tn),
        in_specs=[pl.BlockSpec((tm, k), lambda i, j: (i, 0)),
                  pl.BlockSpec((k, tn), lambda i, j: (0, j)),
                  pl.BlockSpec((tm, HEAD_DIM), lambda i, j: (i % tps, 0)),
                  pl.BlockSpec((tm, HEAD_DIM), lambda i, j: (i % tps, 0)),
                  pl.BlockSpec((1, HEAD_DIM), lambda i, j: (0, 0))],
        out_specs=[pl.BlockSpec((None, tm, KV_WIDTH), lambda i, j: (i // tps, i % tps, 0)),
                   pl.BlockSpec((None, KV_WIDTH, tm), lambda i, j: (i // tps, 0, i % tps)),
                   pl.BlockSpec((tm, tn), z_block)],
        out_shape=[jax.ShapeDtypeStruct((b, s, KV_WIDTH), BF16),
                   jax.ShapeDtypeStruct((b, KV_WIDTH, s), BF16),
                   jax.ShapeDtypeStruct((t, n - tn), BF16)],
        compiler_params=_params(2),
        name="in_proj",
    )(h, w, cosf, sinf, k_norm.reshape(1, HEAD_DIM))


MAX_STATIC_PAIRS = 8


def _attn_kernel(q_ref, qn_ref, cos_ref, sin_ref, cosn_ref, sinn_ref, qg_ref, k_ref, vt_ref, o_ref,
                 qt_scr, st_scr, p_scr, acc_scr, *, tk):
    tq = q_ref.shape[0]
    n_pairs = k_ref.shape[0] // (2 * tk)
    n_chunks = 2 * n_pairs

    def stage_queries(src_ref, c_ref, s_ref):
        for g in range(GQA_GROUP):
            x = src_ref[:, g * HEAD_DIM:(g + 1) * HEAD_DIM].astype(F32)
            q = _norm_rope(x, qg_ref[...], c_ref[...], s_ref[...]) * Q_SCALE
            qt_scr[:, g * tq:(g + 1) * tq] = q.T.astype(qt_scr.dtype)

    def scores(c, slot):
        start = pl.multiple_of(c * tk, tk)
        st_scr[slot] = jnp.dot(k_ref[pl.ds(start, tk), :], qt_scr[...],
                               preferred_element_type=F32)

    def weighted_values(c, slot, alpha):
        start = pl.multiple_of(c * tk, tk)
        acc_scr[...] = alpha * acc_scr[...] + jnp.dot(
            vt_ref[:, pl.ds(start, tk)], p_scr[slot], preferred_element_type=F32)

    def softmax(slot, m_old, l_old):
        st = st_scr[slot]
        m_new = jnp.maximum(m_old, jnp.max(st, axis=0, keepdims=True))
        alpha = jnp.exp2(m_old - m_new)
        p = jnp.exp2(st - m_new)
        p_scr[slot] = p.astype(p_scr.dtype)
        return m_new, alpha * l_old + jnp.sum(p, axis=0, keepdims=True), alpha

    def step(c, slot, carry, has_prev):
        m, l, alpha_prev = carry
        if has_prev:
            weighted_values(c - 1, 1 - slot, alpha_prev)
        if isinstance(c, int) and c == n_chunks - 1:
            stage_queries(qn_ref, cosn_ref, sinn_ref)
            scores(0, 0)
        else:
            scores(c + 1, 1 - slot)
        return softmax(slot, m, l)

    def pair(c, carry, first):
        carry = step(c, 0, carry, not first)
        return step(c + 1, 1, carry, True)

    @pl.when(pl.program_id(2) == 0)
    def _():
        stage_queries(q_ref, cos_ref, sin_ref)
        scores(0, 0)

    acc_scr[...] = jnp.zeros(acc_scr.shape, F32)
    rows = acc_scr.shape[1]
    zero = jnp.zeros((1, rows), F32)
    carry = (jnp.full((1, rows), -jnp.inf, F32), zero, zero)
    if n_pairs <= MAX_STATIC_PAIRS:
        for j in range(n_pairs):
            carry = pair(2 * j, carry, j == 0)
    else:
        carry = pair(0, carry, True)
        carry = lax.fori_loop(1, n_pairs - 1, lambda j, cr: pair(2 * j, cr, False), carry)
        carry = pair(n_chunks - 2, carry, False)
    _, l_fin, alpha_last = carry
    weighted_values(n_chunks - 1, 1, alpha_last)
    out = acc_scr[...] * (1.0 / l_fin)
    for g in range(GQA_GROUP):
        o_ref[:, g * HEAD_DIM:(g + 1) * HEAD_DIM] = out[:, g * tq:(g + 1) * tq].T.astype(o_ref.dtype)


def _attention(z3, q_norm, cosf, sinf, kr, vt):
    b, s, _ = kr.shape
    tq = _tile(s, 256)
    tk = _tile(s // 2, 512)
    rows = GQA_GROUP * tq
    gw = GQA_GROUP * HEAD_DIM
    n_tiles = s // tq

    def nxt(i):
        return jnp.minimum(i + 1, n_tiles - 1)

    return pl.pallas_call(
        functools.partial(_attn_kernel, tk=tk),
        grid=(b, N_KV_HEADS, n_tiles),
        in_specs=[pl.BlockSpec((None, tq, gw), lambda bi, h, i: (bi, i, h)),
                  pl.BlockSpec((None, tq, gw), lambda bi, h, i: (bi, nxt(i), h)),
                  pl.BlockSpec((tq, HEAD_DIM), lambda bi, h, i: (i, 0)),
                  pl.BlockSpec((tq, HEAD_DIM), lambda bi, h, i: (i, 0)),
                  pl.BlockSpec((tq, HEAD_DIM), lambda bi, h, i: (nxt(i), 0)),
                  pl.BlockSpec((tq, HEAD_DIM), lambda bi, h, i: (nxt(i), 0)),
                  pl.BlockSpec((1, HEAD_DIM), lambda bi, h, i: (0, 0)),
                  pl.BlockSpec((None, s, HEAD_DIM), lambda bi, h, i: (bi, 0, h)),
                  pl.BlockSpec((None, HEAD_DIM, s), lambda bi, h, i: (bi, h, 0))],
        out_specs=pl.BlockSpec((None, tq, gw), lambda bi, h, i: (bi, i, h)),
        out_shape=jax.ShapeDtypeStruct((b, s, ATTN_WIDTH), BF16),
        scratch_shapes=[pltpu.VMEM((HEAD_DIM, rows), BF16),
                        pltpu.VMEM((2, tk, rows), F32),
                        pltpu.VMEM((2, tk, rows), BF16),
                        pltpu.VMEM((HEAD_DIM, rows), F32)],
        compiler_params=_params(3),
        name="gqa_attention",
    )(z3, z3, cosf, sinf, cosf, sinf, q_norm.reshape(1, HEAD_DIM), kr, vt)


def _softplus(x):
    return jnp.maximum(x, 0.0) + jnp.log1p(jnp.exp(-jnp.abs(x)))


def _gelu_tanh(x):
    c = math.sqrt(2.0 / math.pi)
    return 0.5 * x * (1.0 + jnp.tanh(c * (x + 0.044715 * (x * x * x))))


def _lru_kernel(xr_ref, yr_ref, cw_ref, cb_ref, wg_ref, bg_ref, lam_ref, o_ref,
                xpad, hf, hb, *, chunk):
    s = xr_ref.shape[0]
    n_chunks = s // chunk
    n_tiles = chunk // 8
    w = LANES

    xpad[0:HALO, :] = jnp.zeros((HALO, w), F32)
    xpad[s + HALO:s + 2 * HALO, :] = jnp.zeros((HALO, w), F32)

    def fill(c, carry):
        t0 = pl.multiple_of(c * chunk, chunk)
        xpad[pl.ds(t0 + HALO, chunk), :] = xr_ref[pl.ds(t0, chunk), :].astype(F32)
        return carry

    lax.fori_loop(0, n_chunks, fill, 0)

    cw = cw_ref[...]
    cb = cb_ref[...]
    neg_c_sp = -LRU_C * _softplus(-lam_ref[...])
    sub = lax.broadcasted_iota(jnp.int32, (1, 8, w), 1)
    ext = chunk + 2 * HALO

    def conv(t0):
        out = cb
        for j in range(CONV_W):
            out = out + cw[j:j + 1] * xpad[pl.ds(t0 + (HALO - 1 + j), chunk), :]
        return out

    def gate_terms(xc, direction):
        col = 2 * w * direction
        g = jnp.dot(xc.astype(BF16), wg_ref[:, col:col + 2 * w],
                    preferred_element_type=F32) + bg_ref[:, col:col + 2 * w]
        r = jax.nn.sigmoid(g[:, :w])
        i = jax.nn.sigmoid(g[:, w:])
        log_a = r * neg_c_sp[direction:direction + 1]
        a = jnp.exp(log_a)
        m = -jnp.tanh(log_a) * (a * a + 1.0)
        root = m * lax.rsqrt(jnp.maximum(m, F32_TINY))
        u = root * (i * xc)
        return a.reshape(n_tiles, 8, w), u.reshape(n_tiles, 8, w)

    def tile_scan(a, u, reverse):
        for d in (1, 2, 4):
            shift = (8 - d) if reverse else d
            keep = (sub < 8 - d) if reverse else (sub >= d)
            a_sh = pltpu.roll(a, shift, 1)
            u_sh = pltpu.roll(u, shift, 1)
            u = jnp.where(keep, u + a * u_sh, u)
            a = jnp.where(keep, a * a_sh, a)
        return a, u

    def body(c, carry):
        h_f, h_b = carry
        tf = pl.multiple_of(c * chunk, chunk)
        tb = pl.multiple_of((n_chunks - 1 - c) * chunk, chunk)
        a_f, u_f = tile_scan(*gate_terms(conv(tf), 0), reverse=False)
        a_b, u_b = tile_scan(*gate_terms(conv(tb), 1), reverse=True)
        for k in range(n_tiles):
            ht = u_f[k] + a_f[k] * h_f
            hf[pl.ds(tf + 8 * k, 8), :] = ht
            h_f = jnp.broadcast_to(ht[7:8, :], (8, w))
            kb = n_tiles - 1 - k
            ht = u_b[kb] + a_b[kb] * h_b
            hb[pl.ds(tb + 8 * kb, 8), :] = ht
            h_b = jnp.broadcast_to(ht[0:1, :], (8, w))
        return h_f, h_b

    zero = jnp.zeros((8, w), F32)
    lax.fori_loop(0, n_chunks, body, (zero, zero))

    def emit(c, carry):
        t0 = pl.multiple_of(c * chunk, chunk)
        y = yr_ref[pl.ds(t0, chunk), :].astype(F32)
        rec = (hf[pl.ds(t0, chunk), :] + hb[pl.ds(t0, chunk), :]) * _gelu_tanh(y)
        o_ref[pl.ds(t0, chunk), :] = rec.astype(o_ref.dtype)
        return carry

    lax.fori_loop(0, n_chunks, emit, 0)


def _lru(z3, conv_w, conv_b, w_a, b_a, w_i, b_i, lam, x_col0, y_col0):
    b, s, _ = z3.shape
    n_blocks = w_a.shape[1]
    d_rnn = n_blocks * LANES
    chunk = _tile(s, 256)
    wg = jnp.concatenate([w_a[0], w_i[0], w_a[1], w_i[1]], axis=-1).astype(BF16)
    bg = jnp.concatenate(
        [v.reshape(n_blocks, 1, LANES) for v in (b_a[0], b_i[0], b_a[1], b_i[1])], axis=-1)
    x_blk = x_col0 // LANES
    y_blk = y_col0 // LANES
    return pl.pallas_call(
        functools.partial(_lru_kernel, chunk=chunk),
        grid=(b, n_blocks),
        in_specs=[pl.BlockSpec((None, s, LANES), lambda bi, j: (bi, 0, x_blk + j)),
                  pl.BlockSpec((None, s, LANES), lambda bi, j: (bi, 0, y_blk + j)),
                  pl.BlockSpec((CONV_W, LANES), lambda bi, j: (0, j)),
                  pl.BlockSpec((1, LANES), lambda bi, j: (0, j)),
                  pl.BlockSpec((None, LANES, 4 * LANES), lambda bi, j: (j, 0, 0)),
                  pl.BlockSpec((None, 1, 4 * LANES), lambda bi, j: (j, 0, 0)),
                  pl.BlockSpec((2, LANES), lambda bi, j: (0, j))],
        out_specs=pl.BlockSpec((None, s, LANES), lambda bi, j: (bi, 0, j)),
        out_shape=jax.ShapeDtypeStruct((b, s, d_rnn), BF16),
        scratch_shapes=[pltpu.VMEM((s + 2 * HALO, LANES), F32),
                        pltpu.VMEM((s, LANES), F32),
                        pltpu.VMEM((s, LANES), F32)],
        compiler_params=_params(2),
        name="conv_rglru",
    )(z3, z3, conv_w, conv_b.reshape(1, d_rnn), wg, bg, lam)


def _encoder_layer(x2, b, s, p):
    d = x2.shape[1]
    d_rnn = p["w_rnn_out"].shape[0]
    c_x = ATTN_WIDTH
    c_y = c_x + d_rnn
    c_g = c_y + d_rnn

    h = _rmsnorm_cast(x2, p["norm_mix"])
    cosf, sinf = _rope_tables(s)
    kr, vt, z = _in_proj(h, p["w_in"], p["k_norm"], cosf, sinf, b, s)
    z3 = z.reshape(b, s, z.shape[1])
    attn = _attention(z3, p["q_norm"], cosf, sinf, kr, vt)
    rec = _lru(z3, p["conv_w"], p["conv_b"], p["lru_w_a"], p["lru_b_a"], p["lru_w_i"],
               p["lru_b_i"], p["lru_lambda"], c_x, c_y)

    merged = _merge(attn.reshape(b * s, ATTN_WIDTH), rec.reshape(b * s, d_rnn),
                    p["w_attn_out"], p["w_rnn_out"], z, p["b_gate"], c_g)
    x1 = _outproj(merged, p["w_out"], x2)
    hm = _rmsnorm_cast(x1, p["norm_mlp"])
    up = _mlp_up(hm, p["w_up"])
    return up, x1


def _trunk(x, layers, norm_final):
    b, s, d = x.shape
    x2 = x.reshape(b * s, d)
    for li, p in enumerate(layers):
        up, x1 = _encoder_layer(x2, b, s, p)
        x2 = _mlp_down(up, p["w_down"], x1)
    return _rmsnorm_cast(x2, norm_final, out_dtype=F32).reshape(b, s, d)


def kernel(x_prompt, x_sample, norm_mix, w_in, q_norm, k_norm, conv_w, conv_b, lru_w_a, lru_b_a,
           lru_w_i, lru_b_i, lru_lambda, w_attn_out, w_rnn_out, b_gate, w_out, norm_mlp, w_up,
           w_down, norm_final):
    depth = w_in.shape[0]
    layers = []
    for l in range(depth):
        layers.append(dict(
            norm_mix=norm_mix[l], w_in=w_in[l].astype(BF16), q_norm=q_norm[l], k_norm=k_norm[l],
            conv_w=conv_w[l], conv_b=conv_b[l], lru_w_a=lru_w_a[l], lru_b_a=lru_b_a[l],
            lru_w_i=lru_w_i[l], lru_b_i=lru_b_i[l], lru_lambda=lru_lambda[l],
            w_attn_out=w_attn_out[l].astype(BF16), w_rnn_out=w_rnn_out[l].astype(BF16),
            b_gate=b_gate[l], w_out=w_out[l].astype(BF16), norm_mlp=norm_mlp[l],
            w_up=w_up[l].astype(BF16), w_down=w_down[l].astype(BF16)))
    y_prompt = _trunk(x_prompt, layers, norm_final)
    y_sample = _trunk(x_sample, layers, norm_final)
    return (y_prompt, y_sample)
```

```python
import functools
import math

import jax
import jax.numpy as jnp
from jax import lax
from jax.experimental import pallas as pl
from jax.experimental.pallas import tpu as pltpu

F32 = jnp.float32
BF16 = jnp.bfloat16

HEAD_DIM = 128
N_Q_HEADS = 16
N_KV_HEADS = 4
GQA_GROUP = N_Q_HEADS // N_KV_HEADS
ATTN_WIDTH = N_Q_HEADS * HEAD_DIM
KV_WIDTH = N_KV_HEADS * HEAD_DIM
ROPE_THETA = 10000.0
GRID_W = 64
LRU_C = 8.0
CONV_W = 4
NORM_EPS = 1e-6
F32_TINY = 1.1754944e-38
LANES = 128
MXU_COLS = 256
HALO = 16
V7X_VMEM_LIMIT = 60 * 1024 * 1024


def _params(n_axes, vmem=V7X_VMEM_LIMIT):
    return pltpu.CompilerParams(
        dimension_semantics=("arbitrary",) * n_axes, vmem_limit_bytes=vmem)


def _tile(n, want):
    t = min(n, want)
    while n % t:
        t //= 2
    return t


def _rmsnorm_cast_kernel(x_ref, g_ref, o_ref):
    x = x_ref[...]
    ms = jnp.mean(x * x, axis=-1, keepdims=True)
    o_ref[...] = (x * lax.rsqrt(ms + NORM_EPS) * g_ref[...]).astype(o_ref.dtype)


def _rmsnorm_cast(x, g, out_dtype=BF16):
    t, d = x.shape
    tr = _tile(t, 256)
    return pl.pallas_call(
        _rmsnorm_cast_kernel,
        grid=(t // tr,),
        in_specs=[pl.BlockSpec((tr, d), lambda i: (i, 0)),
                  pl.BlockSpec((1, d), lambda i: (0, 0))],
        out_specs=pl.BlockSpec((tr, d), lambda i: (i, 0)),
        out_shape=jax.ShapeDtypeStruct((t, d), out_dtype),
        compiler_params=_params(1),
        name="rmsnorm_cast",
    )(x, g.reshape(1, d))


def _mlp_up_kernel(a_ref, w_ref, o_ref):
    y = jnp.dot(a_ref[...], w_ref[...], preferred_element_type=F32)
    r = jnp.maximum(y, 0.0)
    o_ref[...] = (r * r).astype(o_ref.dtype)


def _mlp_up(a, w):
    m, k = a.shape
    n = w.shape[1]
    tm, tn = _tile(m, 1024), _tile(n, 1024)
    return pl.pallas_call(
        _mlp_up_kernel,
        grid=(m // tm, n // tn),
        in_specs=[pl.BlockSpec((tm, k), lambda i, j: (i, 0)),
                  pl.BlockSpec((k, tn), lambda i, j: (0, j))],
        out_specs=pl.BlockSpec((tm, tn), lambda i, j: (i, j)),
        out_shape=jax.ShapeDtypeStruct((m, n), BF16),
        compiler_params=_params(2),
        name="mlp_up",
    )(a, w)


def _merge_kernel(attn_ref, rec_ref, wa_ref, wr_ref, ga_ref, gr_ref, ba_ref, br_ref, o_ref):
    ab = jnp.dot(attn_ref[...], wa_ref[...], preferred_element_type=F32)
    rb = jnp.dot(rec_ref[...], wr_ref[...], preferred_element_type=F32)
    ga = jax.nn.sigmoid(ga_ref[...].astype(F32) + ba_ref[...])
    gr = jax.nn.sigmoid(gr_ref[...].astype(F32) + br_ref[...])
    o_ref[...] = (ga * ab + gr * rb).astype(o_ref.dtype)


def _merge(attn, rec, wa, wr, z, b_gate, gate_col0):
    m, ka = attn.shape
    kr = rec.shape[1]
    n = wa.shape[1]
    tm, tn = _tile(m, 1024), _tile(n, 1024)
    ga_blk = gate_col0 // tn
    gr_blk = (gate_col0 + n) // tn
    return pl.pallas_call(
        _merge_kernel,
        grid=(m // tm, n // tn),
        in_specs=[pl.BlockSpec((tm, ka), lambda i, j: (i, 0)),
                  pl.BlockSpec((tm, kr), lambda i, j: (i, 0)),
                  pl.BlockSpec((ka, tn), lambda i, j: (0, j)),
                  pl.BlockSpec((kr, tn), lambda i, j: (0, j)),
                  pl.BlockSpec((tm, tn), lambda i, j: (i, ga_blk + j)),
                  pl.BlockSpec((tm, tn), lambda i, j: (i, gr_blk + j)),
                  pl.BlockSpec((1, tn), lambda i, j: (0, j)),
                  pl.BlockSpec((1, tn), lambda i, j: (0, j))],
        out_specs=pl.BlockSpec((tm, tn), lambda i, j: (i, j)),
        out_shape=jax.ShapeDtypeStruct((m, n), BF16),
        compiler_params=_params(2),
        name="branch_merge",
    )(attn, rec, wa, wr, z, z, b_gate[0:1], b_gate[1:2])


def _outproj_kernel(a_ref, w_ref, x_ref, o_ref):
    o_ref[...] = x_ref[...] + jnp.dot(a_ref[...], w_ref[...], preferred_element_type=F32)


def _outproj(a, w, x):
    m, k = a.shape
    n = w.shape[1]
    tm, tn = _tile(m, 1024), _tile(n, 1024)
    return pl.pallas_call(
        _outproj_kernel,
        grid=(m // tm, n // tn),
        in_specs=[pl.BlockSpec((tm, k), lambda i, j: (i, 0)),
                  pl.BlockSpec((k, tn), lambda i, j: (0, j)),
                  pl.BlockSpec((tm, tn), lambda i, j: (i, j))],
        out_specs=pl.BlockSpec((tm, tn), lambda i, j: (i, j)),
        out_shape=jax.ShapeDtypeStruct((m, n), F32),
        compiler_params=_params(2),
        name="out_proj",
    )(a, w, x)


def _mlp_down_kernel(u_ref, w_ref, x_ref, o_ref):
    def product():
        return jnp.dot(u_ref[...], w_ref[...], preferred_element_type=F32)

    @pl.when(pl.program_id(2) == 0)
    def _():
        o_ref[...] = x_ref[...] + product()

    @pl.when(pl.program_id(2) != 0)
    def _():
        o_ref[...] += product()


def _mlp_down(u, w, x):
    m, k = u.shape
    n = w.shape[1]
    tm, tn, tk = _tile(m, 1024), _tile(n, 1024), _tile(k, 4096)
    return pl.pallas_call(
        _mlp_down_kernel,
        grid=(m // tm, n // tn, k // tk),
        in_specs=[pl.BlockSpec((tm, tk), lambda i, j, kk: (i, kk)),
                  pl.BlockSpec((tk, tn), lambda i, j, kk: (kk, j)),
                  pl.BlockSpec((tm, tn), lambda i, j, kk: (i, j))],
        out_specs=pl.BlockSpec((tm, tn), lambda i, j, kk: (i, j)),
        out_shape=jax.ShapeDtypeStruct((m, n), F32),
        compiler_params=_params(3),
        name="mlp_down",
    )(u, w, x)


Q_SCALE = math.log2(math.e) * HEAD_DIM ** -0.5


def _rope_tables(seq_len):
    axis_dim = HEAD_DIM // 2
    n_rows = seq_len // GRID_W
    row = jnp.repeat(jnp.arange(n_rows), GRID_W).astype(F32)
    col = jnp.tile(jnp.arange(GRID_W), n_rows).astype(F32)
    inv = ROPE_THETA ** (-jnp.arange(0, axis_dim, 2, dtype=F32) / axis_dim)
    ang = jnp.concatenate([row[:, None] * inv, col[:, None] * inv], axis=-1)
    cos, sin = jnp.cos(ang), jnp.sin(ang)
    cosf = jnp.repeat(cos, 2, axis=-1)
    sinf = jnp.stack([-sin, sin], axis=-1).reshape(seq_len, HEAD_DIM)
    return cosf, sinf


IN_PROJ_TN = 2 * KV_WIDTH


def _norm_rope(x, g, cos, sin):
    ms = jnp.mean(x * x, axis=-1, keepdims=True)
    xn = x * lax.rsqrt(ms + NORM_EPS) * g
    even = (lax.broadcasted_iota(jnp.int32, xn.shape, 1) & 1) == 0
    nxt = pltpu.roll(xn, HEAD_DIM - 1, 1)
    prv = pltpu.roll(xn, 1, 1)
    return xn * cos + jnp.where(even, nxt, prv) * sin


def _in_proj_kernel(a_ref, w_ref, cos_ref, sin_ref, kg_ref, ko_ref, vt_ref, z_ref, *, kv_tile):
    j = pl.program_id(1)

    def product():
        return jnp.dot(a_ref[...], w_ref[...], preferred_element_type=F32)

    @pl.when(j == kv_tile)
    def _():
        y = product()
        for h in range(N_KV_HEADS):
            sl = slice(h * HEAD_DIM, (h + 1) * HEAD_DIM)
            ko_ref[:, sl] = _norm_rope(y[:, sl], kg_ref[...], cos_ref[...],
                                       sin_ref[...]).astype(ko_ref.dtype)
            vsl = slice(KV_WIDTH + h * HEAD_DIM, KV_WIDTH + (h + 1) * HEAD_DIM)
            vt_ref[sl, :] = y[:, vsl].T.astype(vt_ref.dtype)

    @pl.when(j != kv_tile)
    def _():
        z_ref[...] = product().astype(z_ref.dtype)


def _in_proj(h, w, k_norm, cosf, sinf, b, s):
    t, k = h.shape
    n = w.shape[1]
    tn = IN_PROJ_TN
    tm = _tile(s, 1024)
    assert ATTN_WIDTH % tn == 0 and n % tn == 0
    kv_tile = ATTN_WIDTH // tn
    tps = s // tm

    def z_block(i, j):
        return i, jnp.where(j < kv_tile, j, j - 1)

    return pl.pallas_call(
        functools.partial(_in_proj_kernel, kv_tile=kv_tile),
        grid=(t // tm, n // tn),
        in_specs=[pl.BlockSpec((tm, k), lambda i, j: (i, 0)),
                  pl.BlockSpec((k, tn), lambda i, j: (0, j)),
                  pl.BlockSpec((tm, HEAD_DIM), lambda i, j: (i % tps, 0)),
                  pl.BlockSpec((tm, HEAD_DIM), lambda i, j: (i % tps, 0)),
                  pl.BlockSpec((1, HEAD_DIM), lambda i, j: (0, 0))],
        out_specs=[pl.BlockSpec((None, tm, KV_WIDTH), lambda i, j: (i // tps, i % tps, 0)),
                   pl.BlockSpec((None, KV_WIDTH, tm), lambda i, j: (i // tps, 0, i % tps)),
                   pl.BlockSpec((tm, tn), z_block)],
        out_shape=[jax.ShapeDtypeStruct((b, s, KV_WIDTH), BF16),
                   jax.ShapeDtypeStruct((b, KV_WIDTH, s), BF16),
                   jax.ShapeDtypeStruct((t, n - tn), BF16)],
        compiler_params=_params(2),
        name="in_proj",
    )(h, w, cosf, sinf, k_norm.reshape(1, HEAD_DIM))


MAX_STATIC_PAIRS = 8


N_ATTN_INPUTS = 9


def _attn_kernel(*refs, tk, cast_blocks):
    n_cast = len(cast_blocks)
    (q_ref, qn_ref, cos_ref, sin_ref, cosn_ref, sinn_ref, qg_ref, k_ref,
     vt_ref) = refs[:N_ATTN_INPUTS]
    cast_in = refs[N_ATTN_INPUTS:N_ATTN_INPUTS + n_cast]
    o_ref = refs[N_ATTN_INPUTS + n_cast]
    cast_out = refs[N_ATTN_INPUTS + n_cast + 1:N_ATTN_INPUTS + 2 * n_cast + 1]
    qt_scr, qn_scr, st_scr, mx_scr, p_scr, acc_scr = refs[N_ATTN_INPUTS + 2 * n_cast + 1:]
    tq = q_ref.shape[0]
    n_pairs = k_ref.shape[0] // (2 * tk)
    n_chunks = 2 * n_pairs

    g_step = ((pl.program_id(0) * pl.num_programs(1) + pl.program_id(1)) * pl.num_programs(2)
              + pl.program_id(2))
    for src, dst, n_blocks in zip(cast_in, cast_out, cast_blocks):
        @pl.when(g_step < n_blocks)
        def _(src=src, dst=dst):
            dst[...] = src[...].astype(dst.dtype)

    def stage_queries(src_ref, c_ref, s_ref, dst_scr):
        for g in range(GQA_GROUP):
            x = src_ref[:, g * HEAD_DIM:(g + 1) * HEAD_DIM].astype(F32)
            q = _norm_rope(x, qg_ref[...], c_ref[...], s_ref[...]) * Q_SCALE
            dst_scr[:, g * tq:(g + 1) * tq] = q.T.astype(dst_scr.dtype)

    def scores(c, slot, q_scr=qt_scr):
        start = pl.multiple_of(c * tk, tk)
        st = jnp.dot(k_ref[pl.ds(start, tk), :], q_scr[...],
                     preferred_element_type=F32)
        st_scr[slot] = st
        mx_scr[slot] = jnp.max(st, axis=0, keepdims=True)

    def weighted_values(c, slot, alpha):
        start = pl.multiple_of(c * tk, tk)
        acc_scr[...] = alpha * acc_scr[...] + jnp.dot(
            vt_ref[:, pl.ds(start, tk)], p_scr[slot], preferred_element_type=F32)

    def softmax(slot, m_old, l_old):
        st = st_scr[slot]
        m_new = jnp.maximum(m_old, mx_scr[slot])
        alpha = jnp.exp2(m_old - m_new)
        p = jnp.exp2(st - m_new)
        p_scr[slot] = p.astype(p_scr.dtype)
        return m_new, alpha * l_old + jnp.sum(p, axis=0, keepdims=True), alpha

    def step(c, slot, carry, has_prev):
        m, l, alpha_prev = carry
        if has_prev:
            weighted_values(c - 1, 1 - slot, alpha_prev)
        if isinstance(c, int) and c == n_chunks - 1:
            scores(0, 0, qn_scr)
        else:
            scores(c + 1, 1 - slot)
        return softmax(slot, m, l)

    def pair(c, carry, first):
        carry = step(c, 0, carry, not first)
        return step(c + 1, 1, carry, True)

    @pl.when(pl.program_id(2) == 0)
    def _():
        stage_queries(q_ref, cos_ref, sin_ref, qt_scr)
        scores(0, 0)

    @pl.when(pl.program_id(2) != 0)
    def _():
        qt_scr[...] = qn_scr[...]

    stage_queries(qn_ref, cosn_ref, sinn_ref, qn_scr)
    acc_scr[...] = jnp.zeros(acc_scr.shape, F32)
    rows = acc_scr.shape[1]
    zero = jnp.zeros((1, rows), F32)
    carry = (jnp.full((1, rows), -jnp.inf, F32), zero, zero)
    if n_pairs <= MAX_STATIC_PAIRS:
        for j in range(n_pairs):
            carry = pair(2 * j, carry, j == 0)
    else:
        carry = pair(0, carry, True)
        carry = lax.fori_loop(1, n_pairs - 1, lambda j, cr: pair(2 * j, cr, False), carry)
        carry = pair(n_chunks - 2, carry, False)
    _, l_fin, alpha_last = carry
    weighted_values(n_chunks - 1, 1, alpha_last)
    out = acc_scr[...] * (1.0 / l_fin)
    for g in range(GQA_GROUP):
        o_ref[:, g * HEAD_DIM:(g + 1) * HEAD_DIM] = out[:, g * tq:(g + 1) * tq].T.astype(o_ref.dtype)


CAST_BLOCK = (256, 1024)


def _attention(z3, q_norm, cosf, sinf, kr, vt, cast=()):
    b, s, _ = kr.shape
    tq = _tile(s, 256)
    tk = _tile(s // 2, 512)
    rows = GQA_GROUP * tq
    gw = GQA_GROUP * HEAD_DIM
    n_tiles = s // tq
    n_steps = b * N_KV_HEADS * n_tiles

    def nxt(i):
        return jnp.minimum(i + 1, n_tiles - 1)

    cast_specs, cast_shapes, cast_blocks = [], [], []
    for w in cast:
        br, bc = _tile(w.shape[0], CAST_BLOCK[0]), _tile(w.shape[1], CAST_BLOCK[1])
        while (w.shape[0] // br) * (w.shape[1] // bc) > n_steps:
            br *= 2
        n_bc = w.shape[1] // bc
        n_blocks = (w.shape[0] // br) * n_bc

        def block(bi, h, i, n_bc=n_bc, n_blocks=n_blocks):
            g = jnp.minimum((bi * N_KV_HEADS + h) * n_tiles + i, n_blocks - 1)
            return g // n_bc, g % n_bc

        cast_specs.append(pl.BlockSpec((br, bc), block))
        cast_shapes.append(jax.ShapeDtypeStruct(w.shape, BF16))
        cast_blocks.append(n_blocks)

    outs = pl.pallas_call(
        functools.partial(_attn_kernel, tk=tk, cast_blocks=tuple(cast_blocks)),
        grid=(b, N_KV_HEADS, n_tiles),
        in_specs=[pl.BlockSpec((None, tq, gw), lambda bi, h, i: (bi, i, h)),
                  pl.BlockSpec((None, tq, gw), lambda bi, h, i: (bi, nxt(i), h)),
                  pl.BlockSpec((tq, HEAD_DIM), lambda bi, h, i: (i, 0)),
                  pl.BlockSpec((tq, HEAD_DIM), lambda bi, h, i: (i, 0)),
                  pl.BlockSpec((tq, HEAD_DIM), lambda bi, h, i: (nxt(i), 0)),
                  pl.BlockSpec((tq, HEAD_DIM), lambda bi, h, i: (nxt(i), 0)),
                  pl.BlockSpec((1, HEAD_DIM), lambda bi, h, i: (0, 0)),
                  pl.BlockSpec((None, s, HEAD_DIM), lambda bi, h, i: (bi, 0, h)),
                  pl.BlockSpec((None, HEAD_DIM, s), lambda bi, h, i: (bi, h, 0))] + cast_specs,
        out_specs=[pl.BlockSpec((None, tq, gw), lambda bi, h, i: (bi, i, h))] + cast_specs,
        out_shape=[jax.ShapeDtypeStruct((b, s, ATTN_WIDTH), BF16)] + cast_shapes,
        scratch_shapes=[pltpu.VMEM((HEAD_DIM, rows), BF16),
                        pltpu.VMEM((HEAD_DIM, rows), BF16),
                        pltpu.VMEM((2, tk, rows), F32),
                        pltpu.VMEM((2, 1, rows), F32),
                        pltpu.VMEM((2, tk, rows), BF16),
                        pltpu.VMEM((HEAD_DIM, rows), F32)],
        compiler_params=_params(3),
        name="gqa_attention",
    )(z3, z3, cosf, sinf, cosf, sinf, q_norm.reshape(1, HEAD_DIM), kr, vt, *cast)
    return outs[0], tuple(outs[1:])


def _softplus(x):
    return jnp.maximum(x, 0.0) + jnp.log1p(jnp.exp(-jnp.abs(x)))


def _gelu_tanh(x):
    c = math.sqrt(2.0 / math.pi)
    return 0.5 * x * (1.0 + jnp.tanh(c * (x + 0.044715 * (x * x * x))))


def _lru_kernel(xr_ref, yr_ref, cw_ref, cb_ref, wg_ref, bg_ref, lam_ref, o_ref,
                xpad, hf, hb, *, chunk):
    s = xr_ref.shape[0]
    n_chunks = s // chunk
    n_tiles = chunk // 8
    w = LANES

    xpad[0:HALO, :] = jnp.zeros((HALO, w), F32)
    xpad[s + HALO:s + 2 * HALO, :] = jnp.zeros((HALO, w), F32)

    def fill(c, carry):
        t0 = pl.multiple_of(c * chunk, chunk)
        xpad[pl.ds(t0 + HALO, chunk), :] = xr_ref[pl.ds(t0, chunk), :].astype(F32)
        return carry

    lax.fori_loop(0, n_chunks, fill, 0)

    cw = cw_ref[...]
    cb = cb_ref[...]
    neg_c_sp = -LRU_C * _softplus(-lam_ref[...])
    sub = lax.broadcasted_iota(jnp.int32, (1, 8, w), 1)
    ext = chunk + 2 * HALO

    def conv(t0):
        out = cb
        for j in range(CONV_W):
            out = out + cw[j:j + 1] * xpad[pl.ds(t0 + (HALO - 1 + j), chunk), :]
        return out

    def gate_terms(xc, direction):
        col = 2 * w * direction
        g = jnp.dot(xc.astype(BF16), wg_ref[:, col:col + 2 * w],
                    preferred_element_type=F32) + bg_ref[:, col:col + 2 * w]
        r = jax.nn.sigmoid(g[:, :w])
        i = jax.nn.sigmoid(g[:, w:])
        log_a = r * neg_c_sp[direction:direction + 1]
        a = jnp.exp(log_a)
        m = -jnp.tanh(log_a) * (a * a + 1.0)
        root = m * lax.rsqrt(jnp.maximum(m, F32_TINY))
        u = root * (i * xc)
        return a.reshape(n_tiles, 8, w), u.reshape(n_tiles, 8, w)

    def tile_scan(a, u, reverse):
        for d in (1, 2, 4):
            shift = (8 - d) if reverse else d
            keep = (sub < 8 - d) if reverse else (sub >= d)
            a_sh = pltpu.roll(a, shift, 1)
            u_sh = pltpu.roll(u, shift, 1)
            u = jnp.where(keep, u + a * u_sh, u)
            a = jnp.where(keep, a * a_sh, a)
        return a, u

    def body(c, carry):
        h_f, h_b = carry
        tf = pl.multiple_of(c * chunk, chunk)
        tb = pl.multiple_of((n_chunks - 1 - c) * chunk, chunk)
        a_f, u_f = tile_scan(*gate_terms(conv(tf), 0), reverse=False)
        a_b, u_b = tile_scan(*gate_terms(conv(tb), 1), reverse=True)
        for k in range(n_tiles):
            ht = u_f[k] + a_f[k] * h_f
            hf[pl.ds(tf + 8 * k, 8), :] = ht
            h_f = jnp.broadcast_to(ht[7:8, :], (8, w))
            kb = n_tiles - 1 - k
            ht = u_b[kb] + a_b[kb] * h_b
            hb[pl.ds(tb + 8 * kb, 8), :] = ht
            h_b = jnp.broadcast_to(ht[0:1, :], (8, w))
        return h_f, h_b

    zero = jnp.zeros((8, w), F32)
    lax.fori_loop(0, n_chunks, body, (zero, zero))

    def emit(c, carry):
        t0 = pl.multiple_of(c * chunk, chunk)
        y = yr_ref[pl.ds(t0, chunk), :].astype(F32)
        rec = (hf[pl.ds(t0, chunk), :] + hb[pl.ds(t0, chunk), :]) * _gelu_tanh(y)
        o_ref[pl.ds(t0, chunk), :] = rec.astype(o_ref.dtype)
        return carry

    lax.fori_loop(0, n_chunks, emit, 0)


def _lru(z3, conv_w, conv_b, w_a, b_a, w_i, b_i, lam, x_col0, y_col0):
    b, s, _ = z3.shape
    n_blocks = w_a.shape[1]
    d_rnn = n_blocks * LANES
    chunk = _tile(s, 256)
    wg = jnp.concatenate([w_a[0], w_i[0], w_a[1], w_i[1]], axis=-1).astype(BF16)
    bg = jnp.concatenate(
        [v.reshape(n_blocks, 1, LANES) for v in (b_a[0], b_i[0], b_a[1], b_i[1])], axis=-1)
    x_blk = x_col0 // LANES
    y_blk = y_col0 // LANES
    return pl.pallas_call(
        functools.partial(_lru_kernel, chunk=chunk),
        grid=(b, n_blocks),
        in_specs=[pl.BlockSpec((None, s, LANES), lambda bi, j: (bi, 0, x_blk + j)),
                  pl.BlockSpec((None, s, LANES), lambda bi, j: (bi, 0, y_blk + j)),
                  pl.BlockSpec((CONV_W, LANES), lambda bi, j: (0, j)),
                  pl.BlockSpec((1, LANES), lambda bi, j: (0, j)),
                  pl.BlockSpec((None, LANES, 4 * LANES), lambda bi, j: (j, 0, 0)),
                  pl.BlockSpec((None, 1, 4 * LANES), lambda bi, j: (j, 0, 0)),
                  pl.BlockSpec((2, LANES), lambda bi, j: (0, j))],
        out_specs=pl.BlockSpec((None, s, LANES), lambda bi, j: (bi, 0, j)),
        out_shape=jax.ShapeDtypeStruct((b, s, d_rnn), BF16),
        scratch_shapes=[pltpu.VMEM((s + 2 * HALO, LANES), F32),
                        pltpu.VMEM((s, LANES), F32),
                        pltpu.VMEM((s, LANES), F32)],
        compiler_params=_params(2),
        name="conv_rglru",
    )(z3, z3, conv_w, conv_b.reshape(1, d_rnn), wg, bg, lam)


LATE_WEIGHTS = ("w_attn_out", "w_rnn_out", "w_out", "w_up", "w_down")


def _encoder_layer(x2, b, s, p):
    d_rnn = p["w_rnn_out"].shape[0]
    c_x = ATTN_WIDTH
    c_y = c_x + d_rnn
    c_g = c_y + d_rnn

    h = _rmsnorm_cast(x2, p["norm_mix"])
    cosf, sinf = _rope_tables(s)
    kr, vt, z = _in_proj(h, p["w_in"], p["k_norm"], cosf, sinf, b, s)
    z3 = z.reshape(b, s, z.shape[1])
    pending = tuple(k for k in LATE_WEIGHTS if p[k].dtype != BF16)
    attn, converted = _attention(z3, p["q_norm"], cosf, sinf, kr, vt,
                                 cast=tuple(p[k] for k in pending))
    p.update(zip(pending, converted))
    rec = _lru(z3, p["conv_w"], p["conv_b"], p["lru_w_a"], p["lru_b_a"], p["lru_w_i"],
               p["lru_b_i"], p["lru_lambda"], c_x, c_y)

    merged = _merge(attn.reshape(b * s, ATTN_WIDTH), rec.reshape(b * s, d_rnn),
                    p["w_attn_out"], p["w_rnn_out"], z, p["b_gate"], c_g)
    x1 = _outproj(merged, p["w_out"], x2)
    hm = _rmsnorm_cast(x1, p["norm_mlp"])
    up = _mlp_up(hm, p["w_up"])
    return up, x1


def _trunk(x, layers, norm_final):
    b, s, d = x.shape
    x2 = x.reshape(b * s, d)
    for li, p in enumerate(layers):
        up, x1 = _encoder_layer(x2, b, s, p)
        x2 = _mlp_down(up, p["w_down"], x1)
    return _rmsnorm_cast(x2, norm_final, out_dtype=F32).reshape(b, s, d)


def kernel(x_prompt, x_sample, norm_mix, w_in, q_norm, k_norm, conv_w, conv_b, lru_w_a, lru_b_a,
           lru_w_i, lru_b_i, lru_lambda, w_attn_out, w_rnn_out, b_gate, w_out, norm_mlp, w_up,
           w_down, norm_final):
    depth = w_in.shape[0]
    layers = []
    for l in range(depth):
        layers.append(dict(
            norm_mix=norm_mix[l], w_in=w_in[l].astype(BF16), q_norm=q_norm[l], k_norm=k_norm[l],
            conv_w=conv_w[l], conv_b=conv_b[l], lru_w_a=lru_w_a[l], lru_b_a=lru_b_a[l],
            lru_w_i=lru_w_i[l], lru_b_i=lru_b_i[l], lru_lambda=lru_lambda[l],
            w_attn_out=w_attn_out[l], w_rnn_out=w_rnn_out[l], b_gate=b_gate[l], w_out=w_out[l],
            norm_mlp=norm_mlp[l], w_up=w_up[l], w_down=w_down[l]))
    y_prompt = _trunk(x_prompt, layers, norm_final)
    y_sample = _trunk(x_sample, layers, norm_final)
    return (y_prompt, y_sample)
```

```python
import functools
import math

import jax
import jax.numpy as jnp
from jax import lax
from jax.experimental import pallas as pl
from jax.experimental.pallas import tpu as pltpu

F32 = jnp.float32
BF16 = jnp.bfloat16

HEAD_DIM = 128
N_Q_HEADS = 16
N_KV_HEADS = 4
GQA_GROUP = N_Q_HEADS // N_KV_HEADS
ATTN_WIDTH = N_Q_HEADS * HEAD_DIM
KV_WIDTH = N_KV_HEADS * HEAD_DIM
ROPE_THETA = 10000.0
GRID_W = 64
LRU_C = 8.0
CONV_W = 4
NORM_EPS = 1e-6
F32_TINY = 1.1754944e-38
LANES = 128
MXU_COLS = 256
HALO = 16
V7X_VMEM_LIMIT = 60 * 1024 * 1024


def _params(n_axes, vmem=V7X_VMEM_LIMIT):
    return pltpu.CompilerParams(
        dimension_semantics=("arbitrary",) * n_axes, vmem_limit_bytes=vmem)


def _tile(n, want):
    t = min(n, want)
    while n % t:
        t //= 2
    return t


def _rmsnorm_cast_kernel(x_ref, g_ref, o_ref):
    x = x_ref[...]
    ms = jnp.mean(x * x, axis=-1, keepdims=True)
    o_ref[...] = (x * lax.rsqrt(ms + NORM_EPS) * g_ref[...]).astype(o_ref.dtype)


def _rmsnorm_cast(x, g, out_dtype=BF16):
    t, d = x.shape
    tr = _tile(t, 256)
    return pl.pallas_call(
        _rmsnorm_cast_kernel,
        grid=(t // tr,),
        in_specs=[pl.BlockSpec((tr, d), lambda i: (i, 0)),
                  pl.BlockSpec((1, d), lambda i: (0, 0))],
        out_specs=pl.BlockSpec((tr, d), lambda i: (i, 0)),
        out_shape=jax.ShapeDtypeStruct((t, d), out_dtype),
        compiler_params=_params(1),
        name="rmsnorm_cast",
    )(x, g.reshape(1, d))


def _mlp_up_kernel(a_ref, w_ref, o_ref):
    y = jnp.dot(a_ref[...], w_ref[...], preferred_element_type=F32)
    r = jnp.maximum(y, 0.0)
    o_ref[...] = (r * r).astype(o_ref.dtype)


def _mlp_up(a, w):
    m, k = a.shape
    n = w.shape[1]
    tm, tn = _tile(m, 1024), _tile(n, 1024)
    return pl.pallas_call(
        _mlp_up_kernel,
        grid=(m // tm, n // tn),
        in_specs=[pl.BlockSpec((tm, k), lambda i, j: (i, 0)),
                  pl.BlockSpec((k, tn), lambda i, j: (0, j))],
        out_specs=pl.BlockSpec((tm, tn), lambda i, j: (i, j)),
        out_shape=jax.ShapeDtypeStruct((m, n), BF16),
        compiler_params=_params(2),
        name="mlp_up",
    )(a, w)


def _merge_kernel(attn_ref, rec_ref, wa_ref, wr_ref, ga_ref, gr_ref, ba_ref, br_ref, o_ref):
    ab = jnp.dot(attn_ref[...], wa_ref[...], preferred_element_type=F32)
    rb = jnp.dot(rec_ref[...], wr_ref[...], preferred_element_type=F32)
    ga = jax.nn.sigmoid(ga_ref[...].astype(F32) + ba_ref[...])
    gr = jax.nn.sigmoid(gr_ref[...].astype(F32) + br_ref[...])
    o_ref[...] = (ga * ab + gr * rb).astype(o_ref.dtype)


def _merge(attn, rec, wa, wr, z, b_gate, gate_col0):
    m, ka = attn.shape
    kr = rec.shape[1]
    n = wa.shape[1]
    tm, tn = _tile(m, 1024), _tile(n, 1024)
    ga_blk = gate_col0 // tn
    gr_blk = (gate_col0 + n) // tn
    return pl.pallas_call(
        _merge_kernel,
        grid=(m // tm, n // tn),
        in_specs=[pl.BlockSpec((tm, ka), lambda i, j: (i, 0)),
                  pl.BlockSpec((tm, kr), lambda i, j: (i, 0)),
                  pl.BlockSpec((ka, tn), lambda i, j: (0, j)),
                  pl.BlockSpec((kr, tn), lambda i, j: (0, j)),
                  pl.BlockSpec((tm, tn), lambda i, j: (i, ga_blk + j)),
                  pl.BlockSpec((tm, tn), lambda i, j: (i, gr_blk + j)),
                  pl.BlockSpec((1, tn), lambda i, j: (0, j)),
                  pl.BlockSpec((1, tn), lambda i, j: (0, j))],
        out_specs=pl.BlockSpec((tm, tn), lambda i, j: (i, j)),
        out_shape=jax.ShapeDtypeStruct((m, n), BF16),
        compiler_params=_params(2),
        name="branch_merge",
    )(attn, rec, wa, wr, z, z, b_gate[0:1], b_gate[1:2])


def _outproj_kernel(a_ref, w_ref, x_ref, o_ref):
    o_ref[...] = x_ref[...] + jnp.dot(a_ref[...], w_ref[...], preferred_element_type=F32)


def _outproj(a, w, x):
    m, k = a.shape
    n = w.shape[1]
    tm, tn = _tile(m, 1024), _tile(n, 1024)
    return pl.pallas_call(
        _outproj_kernel,
        grid=(m // tm, n // tn),
        in_specs=[pl.BlockSpec((tm, k), lambda i, j: (i, 0)),
                  pl.BlockSpec((k, tn), lambda i, j: (0, j)),
                  pl.BlockSpec((tm, tn), lambda i, j: (i, j))],
        out_specs=pl.BlockSpec((tm, tn), lambda i, j: (i, j)),
        out_shape=jax.ShapeDtypeStruct((m, n), F32),
        compiler_params=_params(2),
        name="out_proj",
    )(a, w, x)


def _mlp_down_kernel(u_ref, w_ref, x_ref, o_ref):
    def product():
        return jnp.dot(u_ref[...], w_ref[...], preferred_element_type=F32)

    @pl.when(pl.program_id(2) == 0)
    def _():
        o_ref[...] = x_ref[...] + product()

    @pl.when(pl.program_id(2) != 0)
    def _():
        o_ref[...] += product()


def _mlp_down(u, w, x):
    m, k = u.shape
    n = w.shape[1]
    tm, tn, tk = _tile(m, 1024), _tile(n, 1024), _tile(k, 4096)
    return pl.pallas_call(
        _mlp_down_kernel,
        grid=(m // tm, n // tn, k // tk),
        in_specs=[pl.BlockSpec((tm, tk), lambda i, j, kk: (i, kk)),
                  pl.BlockSpec((tk, tn), lambda i, j, kk: (kk, j)),
                  pl.BlockSpec((tm, tn), lambda i, j, kk: (i, j))],
        out_specs=pl.BlockSpec((tm, tn), lambda i, j, kk: (i, j)),
        out_shape=jax.ShapeDtypeStruct((m, n), F32),
        compiler_params=_params(3),
        name="mlp_down",
    )(u, w, x)


Q_SCALE = math.log2(math.e) * HEAD_DIM ** -0.5


def _rope_tables(seq_len):
    axis_dim = HEAD_DIM // 2
    n_rows = seq_len // GRID_W
    row = jnp.repeat(jnp.arange(n_rows), GRID_W).astype(F32)
    col = jnp.tile(jnp.arange(GRID_W), n_rows).astype(F32)
    inv = ROPE_THETA ** (-jnp.arange(0, axis_dim, 2, dtype=F32) / axis_dim)
    ang = jnp.concatenate([row[:, None] * inv, col[:, None] * inv], axis=-1)
    cos, sin = jnp.cos(ang), jnp.sin(ang)
    cosf = jnp.repeat(cos, 2, axis=-1)
    sinf = jnp.stack([-sin, sin], axis=-1).reshape(seq_len, HEAD_DIM)
    return cosf, sinf


IN_PROJ_TN = 2 * KV_WIDTH


def _norm_rope(x, g, cos, sin):
    ms = jnp.mean(x * x, axis=-1, keepdims=True)
    xn = x * lax.rsqrt(ms + NORM_EPS) * g
    even = (lax.broadcasted_iota(jnp.int32, xn.shape, 1) & 1) == 0
    nxt = pltpu.roll(xn, HEAD_DIM - 1, 1)
    prv = pltpu.roll(xn, 1, 1)
    return xn * cos + jnp.where(even, nxt, prv) * sin


def _in_proj_kernel(a_ref, w_ref, cos_ref, sin_ref, kg_ref, ko_ref, vt_ref, z_ref, *, kv_tile):
    j = pl.program_id(1)

    def product():
        return jnp.dot(a_ref[...], w_ref[...], preferred_element_type=F32)

    @pl.when(j == kv_tile)
    def _():
        y = product()
        for h in range(N_KV_HEADS):
            sl = slice(h * HEAD_DIM, (h + 1) * HEAD_DIM)
            ko_ref[:, sl] = _norm_rope(y[:, sl], kg_ref[...], cos_ref[...],
                                       sin_ref[...]).astype(ko_ref.dtype)
            vsl = slice(KV_WIDTH + h * HEAD_DIM, KV_WIDTH + (h + 1) * HEAD_DIM)
            vt_ref[sl, :] = y[:, vsl].T.astype(vt_ref.dtype)

    @pl.when(j != kv_tile)
    def _():
        z_ref[...] = product().astype(z_ref.dtype)


def _in_proj(h, w, k_norm, cosf, sinf, b, s):
    t, k = h.shape
    n = w.shape[1]
    tn = IN_PROJ_TN
    tm = _tile(s, 1024)
    assert ATTN_WIDTH % tn == 0 and n % tn == 0
    kv_tile = ATTN_WIDTH // tn
    tps = s // tm

    def z_block(i, j):
        return i, jnp.where(j < kv_tile, j, j - 1)

    return pl.pallas_call(
        functools.partial(_in_proj_kernel, kv_tile=kv_tile),
        grid=(t // tm, n // tn),
        in_specs=[pl.BlockSpec((tm, k), lambda i, j: (i, 0)),
                  pl.BlockSpec((k, tn), lambda i, j: (0, j)),
                  pl.BlockSpec((tm, HEAD_DIM), lambda i, j: (i % tps, 0)),
                  pl.BlockSpec((tm, HEAD_DIM), lambda i, j: (i % tps, 0)),
                  pl.BlockSpec((1, HEAD_DIM), lambda i, j: (0, 0))],
        out_specs=[pl.BlockSpec((None, tm, KV_WIDTH), lambda i, j: (i // tps, i % tps, 0)),
                   pl.BlockSpec((None, KV_WIDTH, tm), lambda i, j: (i // tps, 0, i % tps)),
                   pl.BlockSpec((tm, tn), z_block)],
        out_shape=[jax.ShapeDtypeStruct((b, s, KV_WIDTH), BF16),
                   jax.ShapeDtypeStruct((b, KV_WIDTH, s), BF16),
                   jax.ShapeDtypeStruct((t, n - tn), BF16)],
        compiler_params=_params(2),
        name="in_proj",
    )(h, w, cosf, sinf, k_norm.reshape(1, HEAD_DIM))


MAX_STATIC_PAIRS = 8


N_ATTN_INPUTS = 9


def _attn_kernel(*refs, tk, cast_blocks, norm_blocks):
    n_cast, n_norm = len(cast_blocks), len(norm_blocks)
    (q_ref, qn_ref, cos_ref, sin_ref, cosn_ref, sinn_ref, qg_ref, k_ref,
     vt_ref) = refs[:N_ATTN_INPUTS]
    pos = N_ATTN_INPUTS
    cast_in = refs[pos:pos + n_cast]
    pos += n_cast
    norm_in = refs[pos:pos + 2 * n_norm]
    pos += 2 * n_norm
    o_ref = refs[pos]
    cast_out = refs[pos + 1:pos + 1 + n_cast]
    norm_out = refs[pos + 1 + n_cast:pos + 1 + n_cast + n_norm]
    qt_scr, st_scr, p_scr, acc_scr = refs[pos + 1 + n_cast + n_norm:]
    tq = q_ref.shape[0]
    n_pairs = k_ref.shape[0] // (2 * tk)
    n_chunks = 2 * n_pairs

    g_step = ((pl.program_id(0) * pl.num_programs(1) + pl.program_id(1)) * pl.num_programs(2)
              + pl.program_id(2))
    for src, dst, n_blocks in zip(cast_in, cast_out, cast_blocks):
        @pl.when(g_step < n_blocks)
        def _(src=src, dst=dst):
            dst[...] = src[...].astype(dst.dtype)
    for j, (dst, n_blocks) in enumerate(zip(norm_out, norm_blocks)):
        @pl.when(g_step < n_blocks)
        def _(x_ref=norm_in[2 * j], g_ref=norm_in[2 * j + 1], dst=dst):
            _rmsnorm_cast_kernel(x_ref, g_ref, dst)

    def stage_queries(src_ref, c_ref, s_ref):
        for g in range(GQA_GROUP):
            x = src_ref[:, g * HEAD_DIM:(g + 1) * HEAD_DIM].astype(F32)
            q = _norm_rope(x, qg_ref[...], c_ref[...], s_ref[...]) * Q_SCALE
            qt_scr[:, g * tq:(g + 1) * tq] = q.T.astype(qt_scr.dtype)

    def scores(c, slot):
        start = pl.multiple_of(c * tk, tk)
        st_scr[slot] = jnp.dot(k_ref[pl.ds(start, tk), :], qt_scr[...],
                               preferred_element_type=F32)

    def weighted_values(c, slot, alpha):
        start = pl.multiple_of(c * tk, tk)
        acc_scr[...] = alpha * acc_scr[...] + jnp.dot(
            vt_ref[:, pl.ds(start, tk)], p_scr[slot], preferred_element_type=F32)

    def softmax(slot, m_old, l_old):
        st = st_scr[slot]
        m_new = jnp.maximum(m_old, jnp.max(st, axis=0, keepdims=True))
        alpha = jnp.exp2(m_old - m_new)
        p = jnp.exp2(st - m_new)
        p_scr[slot] = p.astype(p_scr.dtype)
        return m_new, alpha * l_old + jnp.sum(p, axis=0, keepdims=True), alpha

    def step(c, slot, carry, has_prev):
        m, l, alpha_prev = carry
        if has_prev:
            weighted_values(c - 1, 1 - slot, alpha_prev)
        if isinstance(c, int) and c == n_chunks - 1:
            stage_queries(qn_ref, cosn_ref, sinn_ref)
            scores(0, 0)
        else:
            scores(c + 1, 1 - slot)
        return softmax(slot, m, l)

    def pair(c, carry, first):
        carry = step(c, 0, carry, not first)
        return step(c + 1, 1, carry, True)

    @pl.when(pl.program_id(2) == 0)
    def _():
        stage_queries(q_ref, cos_ref, sin_ref)
        scores(0, 0)

    acc_scr[...] = jnp.zeros(acc_scr.shape, F32)
    rows = acc_scr.shape[1]
    zero = jnp.zeros((1, rows), F32)
    carry = (jnp.full((1, rows), -jnp.inf, F32), zero, zero)
    if n_pairs <= MAX_STATIC_PAIRS:
        for j in range(n_pairs):
            carry = pair(2 * j, carry, j == 0)
    else:
        carry = pair(0, carry, True)
        carry = lax.fori_loop(1, n_pairs - 1, lambda j, cr: pair(2 * j, cr, False), carry)
        carry = pair(n_chunks - 2, carry, False)
    _, l_fin, alpha_last = carry
    weighted_values(n_chunks - 1, 1, alpha_last)
    out = acc_scr[...] * (1.0 / l_fin)
    for g in range(GQA_GROUP):
        o_ref[:, g * HEAD_DIM:(g + 1) * HEAD_DIM] = out[:, g * tq:(g + 1) * tq].T.astype(o_ref.dtype)


CAST_BLOCK = (256, 1024)


def _attention(z3, q_norm, cosf, sinf, kr, vt, cast=(), norms=()):
    b, s, _ = kr.shape
    tq = _tile(s, 256)
    tk = _tile(s // 2, 512)
    rows = GQA_GROUP * tq
    gw = GQA_GROUP * HEAD_DIM
    n_tiles = s // tq
    n_steps = b * N_KV_HEADS * n_tiles

    def nxt(i):
        return jnp.minimum(i + 1, n_tiles - 1)

    cast_specs, cast_shapes, cast_blocks = [], [], []
    for w in cast:
        br, bc = _tile(w.shape[0], CAST_BLOCK[0]), _tile(w.shape[1], CAST_BLOCK[1])
        while (w.shape[0] // br) * (w.shape[1] // bc) > n_steps:
            br *= 2
        n_bc = w.shape[1] // bc
        n_blocks = (w.shape[0] // br) * n_bc

        def block(bi, h, i, n_bc=n_bc, n_blocks=n_blocks):
            g = jnp.minimum((bi * N_KV_HEADS + h) * n_tiles + i, n_blocks - 1)
            return g // n_bc, g % n_bc

        cast_specs.append(pl.BlockSpec((br, bc), block))
        cast_shapes.append(jax.ShapeDtypeStruct(w.shape, BF16))
        cast_blocks.append(n_blocks)

    norm_in_specs, norm_out_specs, norm_shapes, norm_blocks, norm_args = [], [], [], [], []
    for x, g, dtype in norms:
        t, d = x.shape
        br = _tile(t, max(16, t // n_steps))
        assert br * n_steps >= t
        n_blocks = t // br

        def row_block(bi, h, i, n_blocks=n_blocks):
            return jnp.minimum((bi * N_KV_HEADS + h) * n_tiles + i, n_blocks - 1), 0

        norm_in_specs += [pl.BlockSpec((br, d), row_block),
                          pl.BlockSpec((1, d), lambda bi, h, i: (0, 0))]
        norm_out_specs.append(pl.BlockSpec((br, d), row_block))
        norm_shapes.append(jax.ShapeDtypeStruct((t, d), dtype))
        norm_blocks.append(n_blocks)
        norm_args += [x, g.reshape(1, d)]

    outs = pl.pallas_call(
        functools.partial(_attn_kernel, tk=tk, cast_blocks=tuple(cast_blocks),
                          norm_blocks=tuple(norm_blocks)),
        grid=(b, N_KV_HEADS, n_tiles),
        in_specs=[pl.BlockSpec((None, tq, gw), lambda bi, h, i: (bi, i, h)),
                  pl.BlockSpec((None, tq, gw), lambda bi, h, i: (bi, nxt(i), h)),
                  pl.BlockSpec((tq, HEAD_DIM), lambda bi, h, i: (i, 0)),
                  pl.BlockSpec((tq, HEAD_DIM), lambda bi, h, i: (i, 0)),
                  pl.BlockSpec((tq, HEAD_DIM), lambda bi, h, i: (nxt(i), 0)),
                  pl.BlockSpec((tq, HEAD_DIM), lambda bi, h, i: (nxt(i), 0)),
                  pl.BlockSpec((1, HEAD_DIM), lambda bi, h, i: (0, 0)),
                  pl.BlockSpec((None, s, HEAD_DIM), lambda bi, h, i: (bi, 0, h)),
                  pl.BlockSpec((None, HEAD_DIM, s), lambda bi, h, i: (bi, h, 0))]
        + cast_specs + norm_in_specs,
        out_specs=[pl.BlockSpec((None, tq, gw), lambda bi, h, i: (bi, i, h))]
        + cast_specs + norm_out_specs,
        out_shape=[jax.ShapeDtypeStruct((b, s, ATTN_WIDTH), BF16)] + cast_shapes + norm_shapes,
        scratch_shapes=[pltpu.VMEM((HEAD_DIM, rows), BF16),
                        pltpu.VMEM((2, tk, rows), F32),
                        pltpu.VMEM((2, tk, rows), BF16),
                        pltpu.VMEM((HEAD_DIM, rows), F32)],
        compiler_params=_params(3),
        name="gqa_attention",
    )(z3, z3, cosf, sinf, cosf, sinf, q_norm.reshape(1, HEAD_DIM), kr, vt, *cast, *norm_args)
    n_cast = len(cast)
    return outs[0], tuple(outs[1:1 + n_cast]), tuple(outs[1 + n_cast:])


def _softplus(x):
    return jnp.maximum(x, 0.0) + jnp.log1p(jnp.exp(-jnp.abs(x)))


def _gelu_tanh(x):
    c = math.sqrt(2.0 / math.pi)
    return 0.5 * x * (1.0 + jnp.tanh(c * (x + 0.044715 * (x * x * x))))


def _lru_kernel(xr_ref, yr_ref, cw_ref, cb_ref, wg_ref, bg_ref, lam_ref, o_ref,
                xpad, hf, hb, *, chunk):
    s = xr_ref.shape[0]
    n_chunks = s // chunk
    n_tiles = chunk // 8
    w = LANES

    xpad[0:HALO, :] = jnp.zeros((HALO, w), F32)
    xpad[s + HALO:s + 2 * HALO, :] = jnp.zeros((HALO, w), F32)

    def fill(c, carry):
        t0 = pl.multiple_of(c * chunk, chunk)
        xpad[pl.ds(t0 + HALO, chunk), :] = xr_ref[pl.ds(t0, chunk), :].astype(F32)
        return carry

    lax.fori_loop(0, n_chunks, fill, 0)

    cw = cw_ref[...]
    cb = cb_ref[...]
    neg_c_sp = -LRU_C * _softplus(-lam_ref[...])
    sub = lax.broadcasted_iota(jnp.int32, (1, 8, w), 1)
    ext = chunk + 2 * HALO

    def conv(t0):
        out = cb
        for j in range(CONV_W):
            out = out + cw[j:j + 1] * xpad[pl.ds(t0 + (HALO - 1 + j), chunk), :]
        return out

    def gate_terms(xc, direction):
        col = 2 * w * direction
        g = jnp.dot(xc.astype(BF16), wg_ref[:, col:col + 2 * w],
                    preferred_element_type=F32) + bg_ref[:, col:col + 2 * w]
        r = jax.nn.sigmoid(g[:, :w])
        i = jax.nn.sigmoid(g[:, w:])
        log_a = r * neg_c_sp[direction:direction + 1]
        a = jnp.exp(log_a)
        m = -jnp.tanh(log_a) * (a * a + 1.0)
        root = m * lax.rsqrt(jnp.maximum(m, F32_TINY))
        u = root * (i * xc)
        return a.reshape(n_tiles, 8, w), u.reshape(n_tiles, 8, w)

    def tile_scan(a, u, reverse):
        for d in (1, 2, 4):
            shift = (8 - d) if reverse else d
            keep = (sub < 8 - d) if reverse else (sub >= d)
            a_sh = pltpu.roll(a, shift, 1)
            u_sh = pltpu.roll(u, shift, 1)
            u = jnp.where(keep, u + a * u_sh, u)
            a = jnp.where(keep, a * a_sh, a)
        return a, u

    def body(c, carry):
        h_f, h_b = carry
        tf = pl.multiple_of(c * chunk, chunk)
        tb = pl.multiple_of((n_chunks - 1 - c) * chunk, chunk)
        a_f, u_f = tile_scan(*gate_terms(conv(tf), 0), reverse=False)
        a_b, u_b = tile_scan(*gate_terms(conv(tb), 1), reverse=True)
        for k in range(n_tiles):
            ht = u_f[k] + a_f[k] * h_f
            hf[pl.ds(tf + 8 * k, 8), :] = ht
            h_f = jnp.broadcast_to(ht[7:8, :], (8, w))
            kb = n_tiles - 1 - k
            ht = u_b[kb] + a_b[kb] * h_b
            hb[pl.ds(tb + 8 * kb, 8), :] = ht
            h_b = jnp.broadcast_to(ht[0:1, :], (8, w))
        return h_f, h_b

    zero = jnp.zeros((8, w), F32)
    lax.fori_loop(0, n_chunks, body, (zero, zero))

    def emit(c, carry):
        t0 = pl.multiple_of(c * chunk, chunk)
        y = yr_ref[pl.ds(t0, chunk), :].astype(F32)
        rec = (hf[pl.ds(t0, chunk), :] + hb[pl.ds(t0, chunk), :]) * _gelu_tanh(y)
        o_ref[pl.ds(t0, chunk), :] = rec.astype(o_ref.dtype)
        return carry

    lax.fori_loop(0, n_chunks, emit, 0)


def _lru(z3, conv_w, conv_b, w_a, b_a, w_i, b_i, lam, x_col0, y_col0):
    b, s, _ = z3.shape
    n_blocks = w_a.shape[1]
    d_rnn = n_blocks * LANES
    chunk = _tile(s, 256)
    wg = jnp.concatenate([w_a[0], w_i[0], w_a[1], w_i[1]], axis=-1).astype(BF16)
    bg = jnp.concatenate(
        [v.reshape(n_blocks, 1, LANES) for v in (b_a[0], b_i[0], b_a[1], b_i[1])], axis=-1)
    x_blk = x_col0 // LANES
    y_blk = y_col0 // LANES
    return pl.pallas_call(
        functools.partial(_lru_kernel, chunk=chunk),
        grid=(b, n_blocks),
        in_specs=[pl.BlockSpec((None, s, LANES), lambda bi, j: (bi, 0, x_blk + j)),
                  pl.BlockSpec((None, s, LANES), lambda bi, j: (bi, 0, y_blk + j)),
                  pl.BlockSpec((CONV_W, LANES), lambda bi, j: (0, j)),
                  pl.BlockSpec((1, LANES), lambda bi, j: (0, j)),
                  pl.BlockSpec((None, LANES, 4 * LANES), lambda bi, j: (j, 0, 0)),
                  pl.BlockSpec((None, 1, 4 * LANES), lambda bi, j: (j, 0, 0)),
                  pl.BlockSpec((2, LANES), lambda bi, j: (0, j))],
        out_specs=pl.BlockSpec((None, s, LANES), lambda bi, j: (bi, 0, j)),
        out_shape=jax.ShapeDtypeStruct((b, s, d_rnn), BF16),
        scratch_shapes=[pltpu.VMEM((s + 2 * HALO, LANES), F32),
                        pltpu.VMEM((s, LANES), F32),
                        pltpu.VMEM((s, LANES), F32)],
        compiler_params=_params(2),
        name="conv_rglru",
    )(z3, z3, conv_w, conv_b.reshape(1, d_rnn), wg, bg, lam)


LATE_WEIGHTS = ("w_attn_out", "w_rnn_out", "w_out", "w_up", "w_down")


def _encoder_layer(x2, h, b, s, p, norms=()):
    d_rnn = p["w_rnn_out"].shape[0]
    c_x = ATTN_WIDTH
    c_y = c_x + d_rnn
    c_g = c_y + d_rnn

    if h is None:
        h = _rmsnorm_cast(x2, p["norm_mix"])
    cosf, sinf = _rope_tables(s)
    kr, vt, z = _in_proj(h, p["w_in"], p["k_norm"], cosf, sinf, b, s)
    z3 = z.reshape(b, s, z.shape[1])
    pending = tuple(k for k in LATE_WEIGHTS if p[k].dtype != BF16)
    attn, converted, normed = _attention(z3, p["q_norm"], cosf, sinf, kr, vt,
                                         cast=tuple(p[k] for k in pending), norms=norms)
    p.update(zip(pending, converted))
    rec = _lru(z3, p["conv_w"], p["conv_b"], p["lru_w_a"], p["lru_b_a"], p["lru_w_i"],
               p["lru_b_i"], p["lru_lambda"], c_x, c_y)

    merged = _merge(attn.reshape(b * s, ATTN_WIDTH), rec.reshape(b * s, d_rnn),
                    p["w_attn_out"], p["w_rnn_out"], z, p["b_gate"], c_g)
    x1 = _outproj(merged, p["w_out"], x2)
    hm = _rmsnorm_cast(x1, p["norm_mlp"])
    up = _mlp_up(hm, p["w_up"])
    return up, x1, normed


def _trunks(x_a, x_b, layers, norm_final):
    (b_a, s_a, d), (b_b, s_b, _) = x_a.shape, x_b.shape
    xa, xb = x_a.reshape(b_a * s_a, d), x_b.reshape(b_b * s_b, d)
    h_b = None
    for li, p in enumerate(layers):
        norms = ((xb, p["norm_mix"], BF16),) if li == 0 else ()
        up, x1, normed = _encoder_layer(xa, None, b_a, s_a, p, norms)
        if li == 0:
            h_b = normed[0]
        xa = _mlp_down(up, p["w_down"], x1)
    y_a = None
    for li, p in enumerate(layers):
        last = li == len(layers) - 1
        norms = ((xa, norm_final, F32),) if last else ()
        up, x1, normed = _encoder_layer(xb, h_b if li == 0 else None, b_b, s_b, p, norms)
        if last:
            y_a = normed[0]
        xb = _mlp_down(up, p["w_down"], x1)
    y_b = _rmsnorm_cast(xb, norm_final, out_dtype=F32)
    return y_a.reshape(b_a, s_a, d), y_b.reshape(b_b, s_b, d)


def kernel(x_prompt, x_sample, norm_mix, w_in, q_norm, k_norm, conv_w, conv_b, lru_w_a, lru_b_a,
           lru_w_i, lru_b_i, lru_lambda, w_attn_out, w_rnn_out, b_gate, w_out, norm_mlp, w_up,
           w_down, norm_final):
    depth = w_in.shape[0]
    layers = []
    for l in range(depth):
        layers.append(dict(
            norm_mix=norm_mix[l], w_in=w_in[l].astype(BF16), q_norm=q_norm[l], k_norm=k_norm[l],
            conv_w=conv_w[l], conv_b=conv_b[l], lru_w_a=lru_w_a[l], lru_b_a=lru_b_a[l],
            lru_w_i=lru_w_i[l], lru_b_i=lru_b_i[l], lru_lambda=lru_lambda[l],
            w_attn_out=w_attn_out[l], w_rnn_out=w_rnn_out[l], b_gate=b_gate[l], w_out=w_out[l],
            norm_mlp=norm_mlp[l], w_up=w_up[l], w_down=w_down[l]))
    return _trunks(x_prompt, x_sample, layers, norm_final)
```

```python
import functools
import math

import jax
import jax.numpy as jnp
from jax import lax
from jax.experimental import pallas as pl
from jax.experimental.pallas import tpu as pltpu

F32 = jnp.float32
BF16 = jnp.bfloat16

HEAD_DIM = 128
N_Q_HEADS = 16
N_KV_HEADS = 4
GQA_GROUP = N_Q_HEADS // N_KV_HEADS
ATTN_WIDTH = N_Q_HEADS * HEAD_DIM
KV_WIDTH = N_KV_HEADS * HEAD_DIM
ROPE_THETA = 10000.0
GRID_W = 64
LRU_C = 8.0
CONV_W = 4
NORM_EPS = 1e-6
F32_TINY = 1.1754944e-38
LANES = 128
MXU_COLS = 256
HALO = 16
V7X_VMEM_LIMIT = 60 * 1024 * 1024


def _params(n_axes, vmem=V7X_VMEM_LIMIT):
    return pltpu.CompilerParams(
        dimension_semantics=("arbitrary",) * n_axes, vmem_limit_bytes=vmem)


def _tile(n, want):
    t = min(n, want)
    while n % t:
        t //= 2
    return t


def _rmsnorm_cast_kernel(x_ref, g_ref, o_ref):
    x = x_ref[...]
    ms = jnp.mean(x * x, axis=-1, keepdims=True)
    o_ref[...] = (x * lax.rsqrt(ms + NORM_EPS) * g_ref[...]).astype(o_ref.dtype)


def _rmsnorm_cast(x, g, out_dtype=BF16):
    t, d = x.shape
    tr = _tile(t, 256)
    return pl.pallas_call(
        _rmsnorm_cast_kernel,
        grid=(t // tr,),
        in_specs=[pl.BlockSpec((tr, d), lambda i: (i, 0)),
                  pl.BlockSpec((1, d), lambda i: (0, 0))],
        out_specs=pl.BlockSpec((tr, d), lambda i: (i, 0)),
        out_shape=jax.ShapeDtypeStruct((t, d), out_dtype),
        compiler_params=_params(1),
        name="rmsnorm_cast",
    )(x, g.reshape(1, d))


def _mlp_up_kernel(a_ref, w_ref, o_ref):
    y = jnp.dot(a_ref[...], w_ref[...], preferred_element_type=F32)
    r = jnp.maximum(y, 0.0)
    o_ref[...] = (r * r).astype(o_ref.dtype)


def _mlp_up(a, w):
    m, k = a.shape
    n = w.shape[1]
    tm, tn = _tile(m, 1024), _tile(n, 1024)
    return pl.pallas_call(
        _mlp_up_kernel,
        grid=(m // tm, n // tn),
        in_specs=[pl.BlockSpec((tm, k), lambda i, j: (i, 0)),
                  pl.BlockSpec((k, tn), lambda i, j: (0, j))],
        out_specs=pl.BlockSpec((tm, tn), lambda i, j: (i, j)),
        out_shape=jax.ShapeDtypeStruct((m, n), BF16),
        compiler_params=_params(2),
        name="mlp_up",
    )(a, w)


def _merge_kernel(attn_ref, rec_ref, wa_ref, wr_ref, ga_ref, gr_ref, ba_ref, br_ref, o_ref):
    ab = jnp.dot(attn_ref[...], wa_ref[...], preferred_element_type=F32)
    rb = jnp.dot(rec_ref[...], wr_ref[...], preferred_element_type=F32)
    ga = jax.nn.sigmoid(ga_ref[...].astype(F32) + ba_ref[...])
    gr = jax.nn.sigmoid(gr_ref[...].astype(F32) + br_ref[...])
    o_ref[...] = (ga * ab + gr * rb).astype(o_ref.dtype)


def _merge(attn, rec, wa, wr, z, b_gate, gate_col0):
    m, ka = attn.shape
    kr = rec.shape[1]
    n = wa.shape[1]
    tm, tn = _tile(m, 1024), _tile(n, 1024)
    ga_blk = gate_col0 // tn
    gr_blk = (gate_col0 + n) // tn
    return pl.pallas_call(
        _merge_kernel,
        grid=(m // tm, n // tn),
        in_specs=[pl.BlockSpec((tm, ka), lambda i, j: (i, 0)),
                  pl.BlockSpec((tm, kr), lambda i, j: (i, 0)),
                  pl.BlockSpec((ka, tn), lambda i, j: (0, j)),
                  pl.BlockSpec((kr, tn), lambda i, j: (0, j)),
                  pl.BlockSpec((tm, tn), lambda i, j: (i, ga_blk + j)),
                  pl.BlockSpec((tm, tn), lambda i, j: (i, gr_blk + j)),
                  pl.BlockSpec((1, tn), lambda i, j: (0, j)),
                  pl.BlockSpec((1, tn), lambda i, j: (0, j))],
        out_specs=pl.BlockSpec((tm, tn), lambda i, j: (i, j)),
        out_shape=jax.ShapeDtypeStruct((m, n), BF16),
        compiler_params=_params(2),
        name="branch_merge",
    )(attn, rec, wa, wr, z, z, b_gate[0:1], b_gate[1:2])


def _outproj_kernel(a_ref, w_ref, x_ref, o_ref):
    o_ref[...] = x_ref[...] + jnp.dot(a_ref[...], w_ref[...], preferred_element_type=F32)


def _outproj(a, w, x):
    m, k = a.shape
    n = w.shape[1]
    tm, tn = _tile(m, 1024), _tile(n, 1024)
    return pl.pallas_call(
        _outproj_kernel,
        grid=(m // tm, n // tn),
        in_specs=[pl.BlockSpec((tm, k), lambda i, j: (i, 0)),
                  pl.BlockSpec((k, tn), lambda i, j: (0, j)),
                  pl.BlockSpec((tm, tn), lambda i, j: (i, j))],
        out_specs=pl.BlockSpec((tm, tn), lambda i, j: (i, j)),
        out_shape=jax.ShapeDtypeStruct((m, n), F32),
        compiler_params=_params(2),
        name="out_proj",
    )(a, w, x)


def _mlp_down_kernel(u_ref, w_ref, x_ref, o_ref):
    def product():
        return jnp.dot(u_ref[...], w_ref[...], preferred_element_type=F32)

    @pl.when(pl.program_id(2) == 0)
    def _():
        o_ref[...] = x_ref[...] + product()

    @pl.when(pl.program_id(2) != 0)
    def _():
        o_ref[...] += product()


def _mlp_down(u, w, x):
    m, k = u.shape
    n = w.shape[1]
    tm, tn, tk = _tile(m, 1024), _tile(n, 1024), _tile(k, 4096)
    return pl.pallas_call(
        _mlp_down_kernel,
        grid=(m // tm, n // tn, k // tk),
        in_specs=[pl.BlockSpec((tm, tk), lambda i, j, kk: (i, kk)),
                  pl.BlockSpec((tk, tn), lambda i, j, kk: (kk, j)),
                  pl.BlockSpec((tm, tn), lambda i, j, kk: (i, j))],
        out_specs=pl.BlockSpec((tm, tn), lambda i, j, kk: (i, j)),
        out_shape=jax.ShapeDtypeStruct((m, n), F32),
        compiler_params=_params(3),
        name="mlp_down",
    )(u, w, x)


Q_SCALE = math.log2(math.e) * HEAD_DIM ** -0.5


def _rope_tables(seq_len):
    axis_dim = HEAD_DIM // 2
    n_rows = seq_len // GRID_W
    row = jnp.repeat(jnp.arange(n_rows), GRID_W).astype(F32)
    col = jnp.tile(jnp.arange(GRID_W), n_rows).astype(F32)
    inv = ROPE_THETA ** (-jnp.arange(0, axis_dim, 2, dtype=F32) / axis_dim)
    ang = jnp.concatenate([row[:, None] * inv, col[:, None] * inv], axis=-1)
    cos, sin = jnp.cos(ang), jnp.sin(ang)
    cosf = jnp.repeat(cos, 2, axis=-1)
    sinf = jnp.stack([-sin, sin], axis=-1).reshape(seq_len, HEAD_DIM)
    return cosf, sinf


IN_PROJ_TN = 2 * KV_WIDTH


def _norm_rope(x, g, cos, sin):
    ms = jnp.mean(x * x, axis=-1, keepdims=True)
    xn = x * lax.rsqrt(ms + NORM_EPS) * g
    even = (lax.broadcasted_iota(jnp.int32, xn.shape, 1) & 1) == 0
    nxt = pltpu.roll(xn, HEAD_DIM - 1, 1)
    prv = pltpu.roll(xn, 1, 1)
    return xn * cos + jnp.where(even, nxt, prv) * sin


def _in_proj_kernel(a_ref, w_ref, cos_ref, sin_ref, kg_ref, ko_ref, vt_ref, z_ref, *, kv_tile):
    j = pl.program_id(1)

    def product():
        return jnp.dot(a_ref[...], w_ref[...], preferred_element_type=F32)

    @pl.when(j == kv_tile)
    def _():
        y = product()
        for h in range(N_KV_HEADS):
            sl = slice(h * HEAD_DIM, (h + 1) * HEAD_DIM)
            ko_ref[:, sl] = _norm_rope(y[:, sl], kg_ref[...], cos_ref[...],
                                       sin_ref[...]).astype(ko_ref.dtype)
            vsl = slice(KV_WIDTH + h * HEAD_DIM, KV_WIDTH + (h + 1) * HEAD_DIM)
            vt_ref[sl, :] = y[:, vsl].T.astype(vt_ref.dtype)

    @pl.when(j != kv_tile)
    def _():
        z_ref[...] = product().astype(z_ref.dtype)


def _in_proj(h, w, k_norm, cosf, sinf, b, s):
    t, k = h.shape
    n = w.shape[1]
    tn = IN_PROJ_TN
    tm = _tile(s, 1024)
    assert ATTN_WIDTH % tn == 0 and n % tn == 0
    kv_tile = ATTN_WIDTH // tn
    tps = s // tm

    def z_block(i, j):
        return i, jnp.where(j < kv_tile, j, j - 1)

    return pl.pallas_call(
        functools.partial(_in_proj_kernel, kv_tile=kv_tile),
        grid=(t // tm, n // tn),
        in_specs=[pl.BlockSpec((tm, k), lambda i, j: (i, 0)),
                  pl.BlockSpec((k, tn), lambda i, j: (0, j)),
                  pl.BlockSpec((tm, HEAD_DIM), lambda i, j: (i % tps, 0)),
                  pl.BlockSpec((tm, HEAD_DIM), lambda i, j: (i % tps, 0)),
                  pl.BlockSpec((1, HEAD_DIM), lambda i, j: (0, 0))],
        out_specs=[pl.BlockSpec((None, tm, KV_WIDTH), lambda i, j: (i // tps, i % tps, 0)),
                   pl.BlockSpec((None, KV_WIDTH, tm), lambda i, j: (i // tps, 0, i % tps)),
                   pl.BlockSpec((tm, tn), z_block)],
        out_shape=[jax.ShapeDtypeStruct((b, s, KV_WIDTH), BF16),
                   jax.ShapeDtypeStruct((b, KV_WIDTH, s), BF16),
                   jax.ShapeDtypeStruct((t, n - tn), BF16)],
        compiler_params=_params(2),
        name="in_proj",
    )(h, w, cosf, sinf, k_norm.reshape(1, HEAD_DIM))


MAX_STATIC_PAIRS = 8


N_ATTN_INPUTS = 9


def _attn_kernel(*refs, tk, cast_blocks, norm_blocks):
    n_cast, n_norm = len(cast_blocks), len(norm_blocks)
    (q_ref, qn_ref, cos_ref, sin_ref, cosn_ref, sinn_ref, qg_ref, k_ref,
     vt_ref) = refs[:N_ATTN_INPUTS]
    pos = N_ATTN_INPUTS
    cast_in = refs[pos:pos + n_cast]
    pos += n_cast
    norm_in = refs[pos:pos + 2 * n_norm]
    pos += 2 * n_norm
    o_ref = refs[pos]
    cast_out = refs[pos + 1:pos + 1 + n_cast]
    norm_out = refs[pos + 1 + n_cast:pos + 1 + n_cast + n_norm]
    qt_scr, st_scr, p_scr, acc_scr = refs[pos + 1 + n_cast + n_norm:]
    tq = q_ref.shape[0]
    n_pairs = k_ref.shape[0] // (2 * tk)
    n_chunks = 2 * n_pairs

    g_step = ((pl.program_id(0) * pl.num_programs(1) + pl.program_id(1)) * pl.num_programs(2)
              + pl.program_id(2))
    for src, dst, n_blocks in zip(cast_in, cast_out, cast_blocks):
        @pl.when(g_step < n_blocks)
        def _(src=src, dst=dst):
            dst[...] = src[...].astype(dst.dtype)
    for j, (dst, n_blocks) in enumerate(zip(norm_out, norm_blocks)):
        @pl.when(g_step < n_blocks)
        def _(x_ref=norm_in[2 * j], g_ref=norm_in[2 * j + 1], dst=dst):
            _rmsnorm_cast_kernel(x_ref, g_ref, dst)

    def stage_queries(src_ref, c_ref, s_ref):
        for g in range(GQA_GROUP):
            x = src_ref[:, g * HEAD_DIM:(g + 1) * HEAD_DIM].astype(F32)
            q = _norm_rope(x, qg_ref[...], c_ref[...], s_ref[...]) * Q_SCALE
            qt_scr[:, g * tq:(g + 1) * tq] = q.T.astype(qt_scr.dtype)

    def scores(c, slot):
        start = pl.multiple_of(c * tk, tk)
        st_scr[slot] = jnp.dot(k_ref[pl.ds(start, tk), :], qt_scr[...],
                               preferred_element_type=F32)

    def weighted_values(c, slot, alpha):
        start = pl.multiple_of(c * tk, tk)
        acc_scr[...] = alpha * acc_scr[...] + jnp.dot(
            vt_ref[:, pl.ds(start, tk)], p_scr[slot], preferred_element_type=F32)

    def softmax(slot, m_old, l_old):
        st = st_scr[slot]
        m_new = jnp.maximum(m_old, jnp.max(st, axis=0, keepdims=True))
        alpha = jnp.exp2(m_old - m_new)
        p = jnp.exp2(st - m_new)
        p_scr[slot] = p.astype(p_scr.dtype)
        return m_new, alpha * l_old + jnp.sum(p, axis=0, keepdims=True), alpha

    def step(c, slot, carry, has_prev):
        m, l, alpha_prev = carry
        if has_prev:
            weighted_values(c - 1, 1 - slot, alpha_prev)
        if isinstance(c, int) and c == n_chunks - 1:
            stage_queries(qn_ref, cosn_ref, sinn_ref)
            scores(0, 0)
        else:
            scores(c + 1, 1 - slot)
        return softmax(slot, m, l)

    def pair(c, carry, first):
        carry = step(c, 0, carry, not first)
        return step(c + 1, 1, carry, True)

    @pl.when(pl.program_id(2) == 0)
    def _():
        stage_queries(q_ref, cos_ref, sin_ref)
        scores(0, 0)

    acc_scr[...] = jnp.zeros(acc_scr.shape, F32)
    rows = acc_scr.shape[1]
    zero = jnp.zeros((1, rows), F32)
    carry = (jnp.full((1, rows), -jnp.inf, F32), zero, zero)
    if n_pairs <= MAX_STATIC_PAIRS:
        for j in range(n_pairs):
            carry = pair(2 * j, carry, j == 0)
    else:
        carry = pair(0, carry, True)
        carry = lax.fori_loop(1, n_pairs - 1, lambda j, cr: pair(2 * j, cr, False), carry)
        carry = pair(n_chunks - 2, carry, False)
    _, l_fin, alpha_last = carry
    weighted_values(n_chunks - 1, 1, alpha_last)
    out = acc_scr[...] * (1.0 / l_fin)
    for g in range(GQA_GROUP):
        o_ref[:, g * HEAD_DIM:(g + 1) * HEAD_DIM] = out[:, g * tq:(g + 1) * tq].T.astype(o_ref.dtype)


CAST_BLOCK = (256, 1024)


def _attention(z3, q_norm, cosf, sinf, kr, vt, cast=(), norms=()):
    b, s, _ = kr.shape
    tq = _tile(s, 256)
    tk = _tile(s // 2, 512)
    rows = GQA_GROUP * tq
    gw = GQA_GROUP * HEAD_DIM
    n_tiles = s // tq
    n_steps = b * N_KV_HEADS * n_tiles

    def nxt(i):
        return jnp.minimum(i + 1, n_tiles - 1)

    cast_specs, cast_shapes, cast_blocks = [], [], []
    for w in cast:
        br, bc = _tile(w.shape[0], CAST_BLOCK[0]), _tile(w.shape[1], CAST_BLOCK[1])
        while (w.shape[0] // br) * (w.shape[1] // bc) > n_steps:
            br *= 2
        n_bc = w.shape[1] // bc
        n_blocks = (w.shape[0] // br) * n_bc

        def block(bi, h, i, n_bc=n_bc, n_blocks=n_blocks):
            g = jnp.minimum((bi * N_KV_HEADS + h) * n_tiles + i, n_blocks - 1)
            return g // n_bc, g % n_bc

        cast_specs.append(pl.BlockSpec((br, bc), block))
        cast_shapes.append(jax.ShapeDtypeStruct(w.shape, BF16))
        cast_blocks.append(n_blocks)

    norm_in_specs, norm_out_specs, norm_shapes, norm_blocks, norm_args = [], [], [], [], []
    for x, g, dtype in norms:
        t, d = x.shape
        br = _tile(t, max(16, t // n_steps))
        assert br * n_steps >= t
        n_blocks = t // br

        def row_block(bi, h, i, n_blocks=n_blocks):
            return jnp.minimum((bi * N_KV_HEADS + h) * n_tiles + i, n_blocks - 1), 0

        norm_in_specs += [pl.BlockSpec((br, d), row_block),
                          pl.BlockSpec((1, d), lambda bi, h, i: (0, 0))]
        norm_out_specs.append(pl.BlockSpec((br, d), row_block))
        norm_shapes.append(jax.ShapeDtypeStruct((t, d), dtype))
        norm_blocks.append(n_blocks)
        norm_args += [x, g.reshape(1, d)]

    outs = pl.pallas_call(
        functools.partial(_attn_kernel, tk=tk, cast_blocks=tuple(cast_blocks),
                          norm_blocks=tuple(norm_blocks)),
        grid=(b, N_KV_HEADS, n_tiles),
        in_specs=[pl.BlockSpec((None, tq, gw), lambda bi, h, i: (bi, i, h)),
                  pl.BlockSpec((None, tq, gw), lambda bi, h, i: (bi, nxt(i), h)),
                  pl.BlockSpec((tq, HEAD_DIM), lambda bi, h, i: (i, 0)),
                  pl.BlockSpec((tq, HEAD_DIM), lambda bi, h, i: (i, 0)),
                  pl.BlockSpec((tq, HEAD_DIM), lambda bi, h, i: (nxt(i), 0)),
                  pl.BlockSpec((tq, HEAD_DIM), lambda bi, h, i: (nxt(i), 0)),
                  pl.BlockSpec((1, HEAD_DIM), lambda bi, h, i: (0, 0)),
                  pl.BlockSpec((None, s, HEAD_DIM), lambda bi, h, i: (bi, 0, h)),
                  pl.BlockSpec((None, HEAD_DIM, s), lambda bi, h, i: (bi, h, 0))]
        + cast_specs + norm_in_specs,
        out_specs=[pl.BlockSpec((None, tq, gw), lambda bi, h, i: (bi, i, h))]
        + cast_specs + norm_out_specs,
        out_shape=[jax.ShapeDtypeStruct((b, s, ATTN_WIDTH), BF16)] + cast_shapes + norm_shapes,
        scratch_shapes=[pltpu.VMEM((HEAD_DIM, rows), BF16),
                        pltpu.VMEM((2, tk, rows), F32),
                        pltpu.VMEM((2, tk, rows), BF16),
                        pltpu.VMEM((HEAD_DIM, rows), F32)],
        compiler_params=_params(3),
        name="gqa_attention",
    )(z3, z3, cosf, sinf, cosf, sinf, q_norm.reshape(1, HEAD_DIM), kr, vt, *cast, *norm_args)
    n_cast = len(cast)
    return outs[0], tuple(outs[1:1 + n_cast]), tuple(outs[1 + n_cast:])


def _softplus(x):
    return jnp.maximum(x, 0.0) + jnp.log1p(jnp.exp(-jnp.abs(x)))


def _gelu_tanh(x):
    c = math.sqrt(2.0 / math.pi)
    return 0.5 * x * (1.0 + jnp.tanh(c * (x + 0.044715 * (x * x * x))))


def _lru_kernel(xr_ref, yr_ref, cw_ref, cb_ref, wg_ref, bg_ref, lam_ref, o_ref,
                xpad, hf, hb, *, chunk):
    s = xr_ref.shape[0]
    n_chunks = s // chunk
    n_tiles = chunk // 8
    w = LANES

    xpad[0:HALO, :] = jnp.zeros((HALO, w), F32)
    xpad[s + HALO:s + 2 * HALO, :] = jnp.zeros((HALO, w), F32)

    def fill(c, carry):
        t0 = pl.multiple_of(c * chunk, chunk)
        xpad[pl.ds(t0 + HALO, chunk), :] = xr_ref[pl.ds(t0, chunk), :].astype(F32)
        return carry

    lax.fori_loop(0, n_chunks, fill, 0)

    cw = cw_ref[...]
    cb = cb_ref[...]
    neg_c_sp = -LRU_C * _softplus(-lam_ref[...])
    sub = lax.broadcasted_iota(jnp.int32, (1, 8, w), 1)
    ext = chunk + 2 * HALO

    def conv(t0):
        out = cb
        for j in range(CONV_W):
            out = out + cw[j:j + 1] * xpad[pl.ds(t0 + (HALO - 1 + j), chunk), :]
        return out

    def gate_terms(xc, direction):
        col = 2 * w * direction
        g = jnp.dot(xc.astype(BF16), wg_ref[:, col:col + 2 * w],
                    preferred_element_type=F32) + bg_ref[:, col:col + 2 * w]
        r = jax.nn.sigmoid(g[:, :w])
        i = jax.nn.sigmoid(g[:, w:])
        log_a = r * neg_c_sp[direction:direction + 1]
        a = jnp.exp(log_a)
        m = -jnp.tanh(log_a) * (a * a + 1.0)
        root = m * lax.rsqrt(jnp.maximum(m, F32_TINY))
        u = root * (i * xc)
        return a.reshape(n_tiles, 8, w), u.reshape(n_tiles, 8, w)

    def tile_scan(a, u, reverse):
        for d in (1, 2, 4):
            shift = (8 - d) if reverse else d
            keep = (sub < 8 - d) if reverse else (sub >= d)
            a_sh = pltpu.roll(a, shift, 1)
            u_sh = pltpu.roll(u, shift, 1)
            u = jnp.where(keep, u + a * u_sh, u)
            a = jnp.where(keep, a * a_sh, a)
        return a, u

    def body(c, carry):
        h_f, h_b = carry
        tf = pl.multiple_of(c * chunk, chunk)
        tb = pl.multiple_of((n_chunks - 1 - c) * chunk, chunk)
        a_f, u_f = tile_scan(*gate_terms(conv(tf), 0), reverse=False)
        a_b, u_b = tile_scan(*gate_terms(conv(tb), 1), reverse=True)
        for k in range(n_tiles):
            ht = u_f[k] + a_f[k] * h_f
            hf[pl.ds(tf + 8 * k, 8), :] = ht
            h_f = jnp.broadcast_to(ht[7:8, :], (8, w))
            kb = n_tiles - 1 - k
            ht = u_b[kb] + a_b[kb] * h_b
            hb[pl.ds(tb + 8 * kb, 8), :] = ht
            h_b = jnp.broadcast_to(ht[0:1, :], (8, w))
        return h_f, h_b

    zero = jnp.zeros((8, w), F32)
    lax.fori_loop(0, n_chunks, body, (zero, zero))

    def emit(c, carry):
        t0 = pl.multiple_of(c * chunk, chunk)
        y = yr_ref[pl.ds(t0, chunk), :].astype(F32)
        rec = (hf[pl.ds(t0, chunk), :] + hb[pl.ds(t0, chunk), :]) * _gelu_tanh(y)
        o_ref[pl.ds(t0, chunk), :] = rec.astype(o_ref.dtype)
        return carry

    lax.fori_loop(0, n_chunks, emit, 0)


def _lru(z3, conv_w, conv_b, w_a, b_a, w_i, b_i, lam, x_col0, y_col0):
    b, s, _ = z3.shape
    n_blocks = w_a.shape[1]
    d_rnn = n_blocks * LANES
    chunk = _tile(s, 1024)
    wg = jnp.concatenate([w_a[0], w_i[0], w_a[1], w_i[1]], axis=-1).astype(BF16)
    bg = jnp.concatenate(
        [v.reshape(n_blocks, 1, LANES) for v in (b_a[0], b_i[0], b_a[1], b_i[1])], axis=-1)
    x_blk = x_col0 // LANES
    y_blk = y_col0 // LANES
    return pl.pallas_call(
        functools.partial(_lru_kernel, chunk=chunk),
        grid=(b, n_blocks),
        in_specs=[pl.BlockSpec((None, s, LANES), lambda bi, j: (bi, 0, x_blk + j)),
                  pl.BlockSpec((None, s, LANES), lambda bi, j: (bi, 0, y_blk + j)),
                  pl.BlockSpec((CONV_W, LANES), lambda bi, j: (0, j)),
                  pl.BlockSpec((1, LANES), lambda bi, j: (0, j)),
                  pl.BlockSpec((None, LANES, 4 * LANES), lambda bi, j: (j, 0, 0)),
                  pl.BlockSpec((None, 1, 4 * LANES), lambda bi, j: (j, 0, 0)),
                  pl.BlockSpec((2, LANES), lambda bi, j: (0, j))],
        out_specs=pl.BlockSpec((None, s, LANES), lambda bi, j: (bi, 0, j)),
        out_shape=jax.ShapeDtypeStruct((b, s, d_rnn), BF16),
        scratch_shapes=[pltpu.VMEM((s + 2 * HALO, LANES), F32),
                        pltpu.VMEM((s, LANES), F32),
                        pltpu.VMEM((s, LANES), F32)],
        compiler_params=_params(2),
        name="conv_rglru",
    )(z3, z3, conv_w, conv_b.reshape(1, d_rnn), wg, bg, lam)


LATE_WEIGHTS = ("w_attn_out", "w_rnn_out", "w_out", "w_up", "w_down")


def _encoder_layer(x2, h, b, s, p, norms=()):
    d_rnn = p["w_rnn_out"].shape[0]
    c_x = ATTN_WIDTH
    c_y = c_x + d_rnn
    c_g = c_y + d_rnn

    if h is None:
        h = _rmsnorm_cast(x2, p["norm_mix"])
    cosf, sinf = _rope_tables(s)
    kr, vt, z = _in_proj(h, p["w_in"], p["k_norm"], cosf, sinf, b, s)
    z3 = z.reshape(b, s, z.shape[1])
    pending = tuple(k for k in LATE_WEIGHTS if p[k].dtype != BF16)
    attn, converted, normed = _attention(z3, p["q_norm"], cosf, sinf, kr, vt,
                                         cast=tuple(p[k] for k in pending), norms=norms)
    p.update(zip(pending, converted))
    rec = _lru(z3, p["conv_w"], p["conv_b"], p["lru_w_a"], p["lru_b_a"], p["lru_w_i"],
               p["lru_b_i"], p["lru_lambda"], c_x, c_y)

    merged = _merge(attn.reshape(b * s, ATTN_WIDTH), rec.reshape(b * s, d_rnn),
                    p["w_attn_out"], p["w_rnn_out"], z, p["b_gate"], c_g)
    x1 = _outproj(merged, p["w_out"], x2)
    hm = _rmsnorm_cast(x1, p["norm_mlp"])
    up = _mlp_up(hm, p["w_up"])
    return up, x1, normed


def _trunks(x_a, x_b, layers, norm_final):
    (b_a, s_a, d), (b_b, s_b, _) = x_a.shape, x_b.shape
    xa, xb = x_a.reshape(b_a * s_a, d), x_b.reshape(b_b * s_b, d)
    h_b = None
    for li, p in enumerate(layers):
        norms = ((xb, p["norm_mix"], BF16),) if li == 0 else ()
        up, x1, normed = _encoder_layer(xa, None, b_a, s_a, p, norms)
        if li == 0:
            h_b = normed[0]
        xa = _mlp_down(up, p["w_down"], x1)
    y_a = None
    for li, p in enumerate(layers):
        last = li == len(layers) - 1
        norms = ((xa, norm_final, F32),) if last else ()
        up, x1, normed = _encoder_layer(xb, h_b if li == 0 else None, b_b, s_b, p, norms)
        if last:
            y_a = normed[0]
        xb = _mlp_down(up, p["w_down"], x1)
    y_b = _rmsnorm_cast(xb, norm_final, out_dtype=F32)
    return y_a.reshape(b_a, s_a, d), y_b.reshape(b_b, s_b, d)


def kernel(x_prompt, x_sample, norm_mix, w_in, q_norm, k_norm, conv_w, conv_b, lru_w_a, lru_b_a,
           lru_w_i, lru_b_i, lru_lambda, w_attn_out, w_rnn_out, b_gate, w_out, norm_mlp, w_up,
           w_down, norm_final):
    depth = w_in.shape[0]
    layers = []
    for l in range(depth):
        layers.append(dict(
            norm_mix=norm_mix[l], w_in=w_in[l].astype(BF16), q_norm=q_norm[l], k_norm=k_norm[l],
            conv_w=conv_w[l], conv_b=conv_b[l], lru_w_a=lru_w_a[l], lru_b_a=lru_b_a[l],
            lru_w_i=lru_w_i[l], lru_b_i=lru_b_i[l], lru_lambda=lru_lambda[l],
            w_attn_out=w_attn_out[l], w_rnn_out=w_rnn_out[l], b_gate=b_gate[l], w_out=w_out[l],
            norm_mlp=norm_mlp[l], w_up=w_up[l], w_down=w_down[l]))
    return _trunks(x_prompt, x_sample, layers, norm_final)
```

```python
import functools
import math

import jax
import jax.numpy as jnp
from jax import lax
from jax.experimental import pallas as pl
from jax.experimental.pallas import tpu as pltpu

F32 = jnp.float32
BF16 = jnp.bfloat16

HEAD_DIM = 128
N_Q_HEADS = 16
N_KV_HEADS = 4
GQA_GROUP = N_Q_HEADS // N_KV_HEADS
ATTN_WIDTH = N_Q_HEADS * HEAD_DIM
KV_WIDTH = N_KV_HEADS * HEAD_DIM
ROPE_THETA = 10000.0
GRID_W = 64
LRU_C = 8.0
CONV_W = 4
NORM_EPS = 1e-6
F32_TINY = 1.1754944e-38
LANES = 128
MXU_COLS = 256
HALO = 16
V7X_VMEM_LIMIT = 60 * 1024 * 1024


def _params(n_axes, vmem=V7X_VMEM_LIMIT):
    return pltpu.CompilerParams(
        dimension_semantics=("arbitrary",) * n_axes, vmem_limit_bytes=vmem)


def _tile(n, want):
    t = min(n, want)
    while n % t:
        t //= 2
    return t


def _rmsnorm_cast_kernel(x_ref, g_ref, o_ref):
    x = x_ref[...]
    ms = jnp.mean(x * x, axis=-1, keepdims=True)
    o_ref[...] = (x * lax.rsqrt(ms + NORM_EPS) * g_ref[...]).astype(o_ref.dtype)


def _rmsnorm_cast(x, g, out_dtype=BF16):
    t, d = x.shape
    tr = _tile(t, 256)
    return pl.pallas_call(
        _rmsnorm_cast_kernel,
        grid=(t // tr,),
        in_specs=[pl.BlockSpec((tr, d), lambda i: (i, 0)),
                  pl.BlockSpec((1, d), lambda i: (0, 0))],
        out_specs=pl.BlockSpec((tr, d), lambda i: (i, 0)),
        out_shape=jax.ShapeDtypeStruct((t, d), out_dtype),
        compiler_params=_params(1),
        name="rmsnorm_cast",
    )(x, g.reshape(1, d))


def _mlp_up_kernel(a_ref, w_ref, o_ref):
    y = jnp.dot(a_ref[...], w_ref[...], preferred_element_type=F32)
    r = jnp.maximum(y, 0.0)
    o_ref[...] = (r * r).astype(o_ref.dtype)


def _mlp_up(a, w):
    m, k = a.shape
    n = w.shape[1]
    tm, tn = _tile(m, 1024), _tile(n, 1024)
    return pl.pallas_call(
        _mlp_up_kernel,
        grid=(m // tm, n // tn),
        in_specs=[pl.BlockSpec((tm, k), lambda i, j: (i, 0)),
                  pl.BlockSpec((k, tn), lambda i, j: (0, j))],
        out_specs=pl.BlockSpec((tm, tn), lambda i, j: (i, j)),
        out_shape=jax.ShapeDtypeStruct((m, n), BF16),
        compiler_params=_params(2),
        name="mlp_up",
    )(a, w)


def _merge_kernel(attn_ref, rec_ref, wa_ref, wr_ref, ga_ref, gr_ref, ba_ref, br_ref, o_ref):
    ab = jnp.dot(attn_ref[...], wa_ref[...], preferred_element_type=F32)
    rb = jnp.dot(rec_ref[...], wr_ref[...], preferred_element_type=F32)
    ga = jax.nn.sigmoid(ga_ref[...].astype(F32) + ba_ref[...])
    gr = jax.nn.sigmoid(gr_ref[...].astype(F32) + br_ref[...])
    o_ref[...] = (ga * ab + gr * rb).astype(o_ref.dtype)


def _merge(attn, rec, wa, wr, z, b_gate, gate_col0):
    m, ka = attn.shape
    kr = rec.shape[1]
    n = wa.shape[1]
    tm, tn = _tile(m, 1024), _tile(n, 1024)
    ga_blk = gate_col0 // tn
    gr_blk = (gate_col0 + n) // tn
    return pl.pallas_call(
        _merge_kernel,
        grid=(m // tm, n // tn),
        in_specs=[pl.BlockSpec((tm, ka), lambda i, j: (i, 0)),
                  pl.BlockSpec((tm, kr), lambda i, j: (i, 0)),
                  pl.BlockSpec((ka, tn), lambda i, j: (0, j)),
                  pl.BlockSpec((kr, tn), lambda i, j: (0, j)),
                  pl.BlockSpec((tm, tn), lambda i, j: (i, ga_blk + j)),
                  pl.BlockSpec((tm, tn), lambda i, j: (i, gr_blk + j)),
                  pl.BlockSpec((1, tn), lambda i, j: (0, j)),
                  pl.BlockSpec((1, tn), lambda i, j: (0, j))],
        out_specs=pl.BlockSpec((tm, tn), lambda i, j: (i, j)),
        out_shape=jax.ShapeDtypeStruct((m, n), BF16),
        compiler_params=_params(2),
        name="branch_merge",
    )(attn, rec, wa, wr, z, z, b_gate[0:1], b_gate[1:2])


def _outproj_kernel(a_ref, w_ref, x_ref, o_ref):
    o_ref[...] = x_ref[...] + jnp.dot(a_ref[...], w_ref[...], preferred_element_type=F32)


def _outproj(a, w, x):
    m, k = a.shape
    n = w.shape[1]
    tm, tn = _tile(m, 1024), _tile(n, 1024)
    return pl.pallas_call(
        _outproj_kernel,
        grid=(m // tm, n // tn),
        in_specs=[pl.BlockSpec((tm, k), lambda i, j: (i, 0)),
                  pl.BlockSpec((k, tn), lambda i, j: (0, j)),
                  pl.BlockSpec((tm, tn), lambda i, j: (i, j))],
        out_specs=pl.BlockSpec((tm, tn), lambda i, j: (i, j)),
        out_shape=jax.ShapeDtypeStruct((m, n), F32),
        compiler_params=_params(2),
        name="out_proj",
    )(a, w, x)


def _mlp_down_kernel(u_ref, w_ref, x_ref, o_ref):
    def product():
        return jnp.dot(u_ref[...], w_ref[...], preferred_element_type=F32)

    @pl.when(pl.program_id(2) == 0)
    def _():
        o_ref[...] = x_ref[...] + product()

    @pl.when(pl.program_id(2) != 0)
    def _():
        o_ref[...] += product()


def _mlp_down(u, w, x):
    m, k = u.shape
    n = w.shape[1]
    tm, tn, tk = _tile(m, 1024), _tile(n, 1024), _tile(k, 4096)
    return pl.pallas_call(
        _mlp_down_kernel,
        grid=(m // tm, n // tn, k // tk),
        in_specs=[pl.BlockSpec((tm, tk), lambda i, j, kk: (i, kk)),
                  pl.BlockSpec((tk, tn), lambda i, j, kk: (kk, j)),
                  pl.BlockSpec((tm, tn), lambda i, j, kk: (i, j))],
        out_specs=pl.BlockSpec((tm, tn), lambda i, j, kk: (i, j)),
        out_shape=jax.ShapeDtypeStruct((m, n), F32),
        compiler_params=_params(3),
        name="mlp_down",
    )(u, w, x)


Q_SCALE = math.log2(math.e) * HEAD_DIM ** -0.5


def _rope_tables(seq_len):
    axis_dim = HEAD_DIM // 2
    n_rows = seq_len // GRID_W
    row = jnp.repeat(jnp.arange(n_rows), GRID_W).astype(F32)
    col = jnp.tile(jnp.arange(GRID_W), n_rows).astype(F32)
    inv = ROPE_THETA ** (-jnp.arange(0, axis_dim, 2, dtype=F32) / axis_dim)
    ang = jnp.concatenate([row[:, None] * inv, col[:, None] * inv], axis=-1)
    cos, sin = jnp.cos(ang), jnp.sin(ang)
    cosf = jnp.repeat(cos, 2, axis=-1)
    sinf = jnp.stack([-sin, sin], axis=-1).reshape(seq_len, HEAD_DIM)
    return cosf, sinf


IN_PROJ_TN = 2 * KV_WIDTH


def _norm_rope(x, g, cos, sin):
    ms = jnp.mean(x * x, axis=-1, keepdims=True)
    xn = x * lax.rsqrt(ms + NORM_EPS) * g
    even = (lax.broadcasted_iota(jnp.int32, xn.shape, 1) & 1) == 0
    nxt = pltpu.roll(xn, HEAD_DIM - 1, 1)
    prv = pltpu.roll(xn, 1, 1)
    return xn * cos + jnp.where(even, nxt, prv) * sin


def _in_proj_kernel(a_ref, w_ref, cos_ref, sin_ref, kg_ref, ko_ref, vt_ref, z_ref, *, kv_tile):
    j = pl.program_id(1)

    def product():
        return jnp.dot(a_ref[...], w_ref[...], preferred_element_type=F32)

    @pl.when(j == kv_tile)
    def _():
        y = product()
        for h in range(N_KV_HEADS):
            sl = slice(h * HEAD_DIM, (h + 1) * HEAD_DIM)
            ko_ref[:, sl] = _norm_rope(y[:, sl], kg_ref[...], cos_ref[...],
                                       sin_ref[...]).astype(ko_ref.dtype)
            vsl = slice(KV_WIDTH + h * HEAD_DIM, KV_WIDTH + (h + 1) * HEAD_DIM)
            vt_ref[sl, :] = y[:, vsl].T.astype(vt_ref.dtype)

    @pl.when(j != kv_tile)
    def _():
        z_ref[...] = product().astype(z_ref.dtype)


def _in_proj(h, w, k_norm, cosf, sinf, b, s):
    t, k = h.shape
    n = w.shape[1]
    tn = IN_PROJ_TN
    tm = _tile(s, 1024)
    assert ATTN_WIDTH % tn == 0 and n % tn == 0
    kv_tile = ATTN_WIDTH // tn
    tps = s // tm

    def z_block(i, j):
        return i, jnp.where(j < kv_tile, j, j - 1)

    return pl.pallas_call(
        functools.partial(_in_proj_kernel, kv_tile=kv_tile),
        grid=(t // tm, n // tn),
        in_specs=[pl.BlockSpec((tm, k), lambda i, j: (i, 0)),
                  pl.BlockSpec((k, tn), lambda i, j: (0, j)),
                  pl.BlockSpec((tm, HEAD_DIM), lambda i, j: (i % tps, 0)),
                  pl.BlockSpec((tm, HEAD_DIM), lambda i, j: (i % tps, 0)),
                  pl.BlockSpec((1, HEAD_DIM), lambda i, j: (0, 0))],
        out_specs=[pl.BlockSpec((None, tm, KV_WIDTH), lambda i, j: (i // tps, i % tps, 0)),
                   pl.BlockSpec((None, KV_WIDTH, tm), lambda i, j: (i // tps, 0, i % tps)),
                   pl.BlockSpec((tm, tn), z_block)],
        out_shape=[jax.ShapeDtypeStruct((b, s, KV_WIDTH), BF16),
                   jax.ShapeDtypeStruct((b, KV_WIDTH, s), BF16),
                   jax.ShapeDtypeStruct((t, n - tn), BF16)],
        compiler_params=_params(2),
        name="in_proj",
    )(h, w, cosf, sinf, k_norm.reshape(1, HEAD_DIM))


MAX_STATIC_PAIRS = 8


N_ATTN_INPUTS = 9


def _attn_kernel(*refs, tk, cast_blocks, norm_blocks, n_steps):
    n_cast, n_norm = len(cast_blocks), len(norm_blocks)
    (q_ref, qn_ref, cos_ref, sin_ref, cosn_ref, sinn_ref, qg_ref, k_ref,
     vt_ref) = refs[:N_ATTN_INPUTS]
    pos = N_ATTN_INPUTS
    cast_in = refs[pos:pos + n_cast]
    pos += n_cast
    norm_in = refs[pos:pos + 2 * n_norm]
    pos += 2 * n_norm
    o_ref = refs[pos]
    cast_out = refs[pos + 1:pos + 1 + n_cast]
    norm_out = refs[pos + 1 + n_cast:pos + 1 + n_cast + n_norm]
    qt_scr, st_scr, p_scr, acc_scr = refs[pos + 1 + n_cast + n_norm:]
    tq = q_ref.shape[0]
    n_pairs = k_ref.shape[0] // (2 * tk)
    n_chunks = 2 * n_pairs

    g_step = ((pl.program_id(0) * pl.num_programs(1) + pl.program_id(1)) * pl.num_programs(2)
              + pl.program_id(2))
    every_step = []

    def side_job(n_blocks, body):
        if n_blocks == n_steps:
            every_step.append(body)
        else:
            pl.when(g_step < n_blocks)(body)

    for src, dst, n_blocks in zip(cast_in, cast_out, cast_blocks):
        def cast_block(src=src, dst=dst):
            dst[...] = src[...].astype(dst.dtype)
        side_job(n_blocks, cast_block)
    for j, (dst, n_blocks) in enumerate(zip(norm_out, norm_blocks)):
        def norm_block(x_ref=norm_in[2 * j], g_ref=norm_in[2 * j + 1], dst=dst):
            _rmsnorm_cast_kernel(x_ref, g_ref, dst)
        side_job(n_blocks, norm_block)

    def stage_queries(src_ref, c_ref, s_ref):
        for g in range(GQA_GROUP):
            x = src_ref[:, g * HEAD_DIM:(g + 1) * HEAD_DIM].astype(F32)
            q = _norm_rope(x, qg_ref[...], c_ref[...], s_ref[...]) * Q_SCALE
            qt_scr[:, g * tq:(g + 1) * tq] = q.T.astype(qt_scr.dtype)

    def scores(c, slot):
        start = pl.multiple_of(c * tk, tk)
        st_scr[slot] = jnp.dot(k_ref[pl.ds(start, tk), :], qt_scr[...],
                               preferred_element_type=F32)

    def weighted_values(c, slot, alpha):
        start = pl.multiple_of(c * tk, tk)
        acc_scr[...] = alpha * acc_scr[...] + jnp.dot(
            vt_ref[:, pl.ds(start, tk)], p_scr[slot], preferred_element_type=F32)

    def softmax(slot, m_old, l_old):
        st = st_scr[slot]
        m_new = jnp.maximum(m_old, jnp.max(st, axis=0, keepdims=True))
        alpha = jnp.exp2(m_old - m_new)
        p = jnp.exp2(st - m_new)
        p_scr[slot] = p.astype(p_scr.dtype)
        return m_new, alpha * l_old + jnp.sum(p, axis=0, keepdims=True), alpha

    def step(c, slot, carry, has_prev):
        m, l, alpha_prev = carry
        if has_prev:
            weighted_values(c - 1, 1 - slot, alpha_prev)
        if isinstance(c, int) and c == n_chunks - 1:
            stage_queries(qn_ref, cosn_ref, sinn_ref)
            scores(0, 0)
        else:
            scores(c + 1, 1 - slot)
        return softmax(slot, m, l)

    def pair(c, carry, first):
        carry = step(c, 0, carry, not first)
        return step(c + 1, 1, carry, True)

    @pl.when(pl.program_id(2) == 0)
    def _():
        stage_queries(q_ref, cos_ref, sin_ref)
        scores(0, 0)

    for body in every_step:
        body()
    acc_scr[...] = jnp.zeros(acc_scr.shape, F32)
    rows = acc_scr.shape[1]
    zero = jnp.zeros((1, rows), F32)
    carry = (jnp.full((1, rows), -jnp.inf, F32), zero, zero)
    if n_pairs <= MAX_STATIC_PAIRS:
        for j in range(n_pairs):
            carry = pair(2 * j, carry, j == 0)
    else:
        carry = pair(0, carry, True)
        carry = lax.fori_loop(1, n_pairs - 1, lambda j, cr: pair(2 * j, cr, False), carry)
        carry = pair(n_chunks - 2, carry, False)
    _, l_fin, alpha_last = carry
    weighted_values(n_chunks - 1, 1, alpha_last)
    out = acc_scr[...] * (1.0 / l_fin)
    for g in range(GQA_GROUP):
        o_ref[:, g * HEAD_DIM:(g + 1) * HEAD_DIM] = out[:, g * tq:(g + 1) * tq].T.astype(o_ref.dtype)


CAST_BLOCK = (256, 1024)


def _attention(z3, q_norm, cosf, sinf, kr, vt, cast=(), norms=()):
    b, s, _ = kr.shape
    tq = _tile(s, 256)
    tk = _tile(s // 2, 512)
    rows = GQA_GROUP * tq
    gw = GQA_GROUP * HEAD_DIM
    n_tiles = s // tq
    n_steps = b * N_KV_HEADS * n_tiles

    def nxt(i):
        return jnp.minimum(i + 1, n_tiles - 1)

    cast_specs, cast_shapes, cast_blocks = [], [], []
    for w in cast:
        br, bc = _tile(w.shape[0], CAST_BLOCK[0]), _tile(w.shape[1], CAST_BLOCK[1])
        while (w.shape[0] // br) * (w.shape[1] // bc) > n_steps:
            br *= 2
        n_bc = w.shape[1] // bc
        n_blocks = (w.shape[0] // br) * n_bc

        def block(bi, h, i, n_bc=n_bc, n_blocks=n_blocks):
            g = jnp.minimum((bi * N_KV_HEADS + h) * n_tiles + i, n_blocks - 1)
            return g // n_bc, g % n_bc

        cast_specs.append(pl.BlockSpec((br, bc), block))
        cast_shapes.append(jax.ShapeDtypeStruct(w.shape, BF16))
        cast_blocks.append(n_blocks)

    norm_in_specs, norm_out_specs, norm_shapes, norm_blocks, norm_args = [], [], [], [], []
    for x, g, dtype in norms:
        t, d = x.shape
        br = _tile(t, max(16, t // n_steps))
        assert br * n_steps >= t
        n_blocks = t // br

        def row_block(bi, h, i, n_blocks=n_blocks):
            return jnp.minimum((bi * N_KV_HEADS + h) * n_tiles + i, n_blocks - 1), 0

        norm_in_specs += [pl.BlockSpec((br, d), row_block),
                          pl.BlockSpec((1, d), lambda bi, h, i: (0, 0))]
        norm_out_specs.append(pl.BlockSpec((br, d), row_block))
        norm_shapes.append(jax.ShapeDtypeStruct((t, d), dtype))
        norm_blocks.append(n_blocks)
        norm_args += [x, g.reshape(1, d)]

    outs = pl.pallas_call(
        functools.partial(_attn_kernel, tk=tk, cast_blocks=tuple(cast_blocks),
                          norm_blocks=tuple(norm_blocks), n_steps=n_steps),
        grid=(b, N_KV_HEADS, n_tiles),
        in_specs=[pl.BlockSpec((None, tq, gw), lambda bi, h, i: (bi, i, h)),
                  pl.BlockSpec((None, tq, gw), lambda bi, h, i: (bi, nxt(i), h)),
                  pl.BlockSpec((tq, HEAD_DIM), lambda bi, h, i: (i, 0)),
                  pl.BlockSpec((tq, HEAD_DIM), lambda bi, h, i: (i, 0)),
                  pl.BlockSpec((tq, HEAD_DIM), lambda bi, h, i: (nxt(i), 0)),
                  pl.BlockSpec((tq, HEAD_DIM), lambda bi, h, i: (nxt(i), 0)),
                  pl.BlockSpec((1, HEAD_DIM), lambda bi, h, i: (0, 0)),
                  pl.BlockSpec((None, s, HEAD_DIM), lambda bi, h, i: (bi, 0, h)),
                  pl.BlockSpec((None, HEAD_DIM, s), lambda bi, h, i: (bi, h, 0))]
        + cast_specs + norm_in_specs,
        out_specs=[pl.BlockSpec((None, tq, gw), lambda bi, h, i: (bi, i, h))]
        + cast_specs + norm_out_specs,
        out_shape=[jax.ShapeDtypeStruct((b, s, ATTN_WIDTH), BF16)] + cast_shapes + norm_shapes,
        scratch_shapes=[pltpu.VMEM((HEAD_DIM, rows), BF16),
                        pltpu.VMEM((2, tk, rows), F32),
                        pltpu.VMEM((2, tk, rows), BF16),
                        pltpu.VMEM((HEAD_DIM, rows), F32)],
        compiler_params=_params(3),
        name="gqa_attention",
    )(z3, z3, cosf, sinf, cosf, sinf, q_norm.reshape(1, HEAD_DIM), kr, vt, *cast, *norm_args)
    n_cast = len(cast)
    return outs[0], tuple(outs[1:1 + n_cast]), tuple(outs[1 + n_cast:])


def _softplus(x):
    return jnp.maximum(x, 0.0) + jnp.log1p(jnp.exp(-jnp.abs(x)))


def _gelu_tanh(x):
    c = math.sqrt(2.0 / math.pi)
    return 0.5 * x * (1.0 + jnp.tanh(c * (x + 0.044715 * (x * x * x))))


def _lru_kernel(xr_ref, yr_ref, cw_ref, cb_ref, wg_ref, bg_ref, lam_ref, o_ref,
                xpad, hf, hb, *, chunk):
    s = xr_ref.shape[0]
    n_chunks = s // chunk
    n_tiles = chunk // 8
    w = LANES

    xpad[0:HALO, :] = jnp.zeros((HALO, w), F32)
    xpad[s + HALO:s + 2 * HALO, :] = jnp.zeros((HALO, w), F32)

    def fill(c, carry):
        t0 = pl.multiple_of(c * chunk, chunk)
        xpad[pl.ds(t0 + HALO, chunk), :] = xr_ref[pl.ds(t0, chunk), :].astype(F32)
        return carry

    lax.fori_loop(0, n_chunks, fill, 0)

    cw = cw_ref[...]
    cb = cb_ref[...]
    neg_c_sp = -LRU_C * _softplus(-lam_ref[...])
    sub = lax.broadcasted_iota(jnp.int32, (1, 8, w), 1)
    ext = chunk + 2 * HALO

    def conv(t0):
        out = cb
        for j in range(CONV_W):
            out = out + cw[j:j + 1] * xpad[pl.ds(t0 + (HALO - 1 + j), chunk), :]
        return out

    def gate_terms(xc, direction):
        col = 2 * w * direction
        g = jnp.dot(xc.astype(BF16), wg_ref[:, col:col + 2 * w],
                    preferred_element_type=F32) + bg_ref[:, col:col + 2 * w]
        r = jax.nn.sigmoid(g[:, :w])
        i = jax.nn.sigmoid(g[:, w:])
        log_a = r * neg_c_sp[direction:direction + 1]
        a = jnp.exp(log_a)
        m = -jnp.tanh(log_a) * (a * a + 1.0)
        root = m * lax.rsqrt(jnp.maximum(m, F32_TINY))
        u = root * (i * xc)
        return a.reshape(n_tiles, 8, w), u.reshape(n_tiles, 8, w)

    def tile_scan(a, u, reverse):
        for d in (1, 2, 4):
            shift = (8 - d) if reverse else d
            keep = (sub < 8 - d) if reverse else (sub >= d)
            a_sh = pltpu.roll(a, shift, 1)
            u_sh = pltpu.roll(u, shift, 1)
            u = jnp.where(keep, u + a * u_sh, u)
            a = jnp.where(keep, a * a_sh, a)
        return a, u

    def body(c, carry):
        h_f, h_b = carry
        tf = pl.multiple_of(c * chunk, chunk)
        tb = pl.multiple_of((n_chunks - 1 - c) * chunk, chunk)
        a_f, u_f = tile_scan(*gate_terms(conv(tf), 0), reverse=False)
        a_b, u_b = tile_scan(*gate_terms(conv(tb), 1), reverse=True)
        for k in range(n_tiles):
            ht = u_f[k] + a_f[k] * h_f
            hf[pl.ds(tf + 8 * k, 8), :] = ht
            h_f = jnp.broadcast_to(ht[7:8, :], (8, w))
            kb = n_tiles - 1 - k
            ht = u_b[kb] + a_b[kb] * h_b
            hb[pl.ds(tb + 8 * kb, 8), :] = ht
            h_b = jnp.broadcast_to(ht[0:1, :], (8, w))
        return h_f, h_b

    zero = jnp.zeros((8, w), F32)
    lax.fori_loop(0, n_chunks, body, (zero, zero))

    def emit(c, carry):
        t0 = pl.multiple_of(c * chunk, chunk)
        y = yr_ref[pl.ds(t0, chunk), :].astype(F32)
        rec = (hf[pl.ds(t0, chunk), :] + hb[pl.ds(t0, chunk), :]) * _gelu_tanh(y)
        o_ref[pl.ds(t0, chunk), :] = rec.astype(o_ref.dtype)
        return carry

    lax.fori_loop(0, n_chunks, emit, 0)


def _lru(z3, conv_w, conv_b, w_a, b_a, w_i, b_i, lam, x_col0, y_col0):
    b, s, _ = z3.shape
    n_blocks = w_a.shape[1]
    d_rnn = n_blocks * LANES
    chunk = _tile(s, 2048)
    wg = jnp.concatenate([w_a[0], w_i[0], w_a[1], w_i[1]], axis=-1).astype(BF16)
    bg = jnp.concatenate(
        [v.reshape(n_blocks, 1, LANES) for v in (b_a[0], b_i[0], b_a[1], b_i[1])], axis=-1)
    x_blk = x_col0 // LANES
    y_blk = y_col0 // LANES
    return pl.pallas_call(
        functools.partial(_lru_kernel, chunk=chunk),
        grid=(b, n_blocks),
        in_specs=[pl.BlockSpec((None, s, LANES), lambda bi, j: (bi, 0, x_blk + j)),
                  pl.BlockSpec((None, s, LANES), lambda bi, j: (bi, 0, y_blk + j)),
                  pl.BlockSpec((CONV_W, LANES), lambda bi, j: (0, j)),
                  pl.BlockSpec((1, LANES), lambda bi, j: (0, j)),
                  pl.BlockSpec((None, LANES, 4 * LANES), lambda bi, j: (j, 0, 0)),
                  pl.BlockSpec((None, 1, 4 * LANES), lambda bi, j: (j, 0, 0)),
                  pl.BlockSpec((2, LANES), lambda bi, j: (0, j))],
        out_specs=pl.BlockSpec((None, s, LANES), lambda bi, j: (bi, 0, j)),
        out_shape=jax.ShapeDtypeStruct((b, s, d_rnn), BF16),
        scratch_shapes=[pltpu.VMEM((s + 2 * HALO, LANES), F32),
                        pltpu.VMEM((s, LANES), F32),
                        pltpu.VMEM((s, LANES), F32)],
        compiler_params=_params(2),
        name="conv_rglru",
    )(z3, z3, conv_w, conv_b.reshape(1, d_rnn), wg, bg, lam)


LATE_WEIGHTS = ("w_attn_out", "w_rnn_out", "w_out", "w_up", "w_down")


def _encoder_layer(x2, h, b, s, p, norms=()):
    d_rnn = p["w_rnn_out"].shape[0]
    c_x = ATTN_WIDTH
    c_y = c_x + d_rnn
    c_g = c_y + d_rnn

    if h is None:
        h = _rmsnorm_cast(x2, p["norm_mix"])
    cosf, sinf = _rope_tables(s)
    kr, vt, z = _in_proj(h, p["w_in"], p["k_norm"], cosf, sinf, b, s)
    z3 = z.reshape(b, s, z.shape[1])
    pending = tuple(k for k in LATE_WEIGHTS if p[k].dtype != BF16)
    attn, converted, normed = _attention(z3, p["q_norm"], cosf, sinf, kr, vt,
                                         cast=tuple(p[k] for k in pending), norms=norms)
    p.update(zip(pending, converted))
    rec = _lru(z3, p["conv_w"], p["conv_b"], p["lru_w_a"], p["lru_b_a"], p["lru_w_i"],
               p["lru_b_i"], p["lru_lambda"], c_x, c_y)

    merged = _merge(attn.reshape(b * s, ATTN_WIDTH), rec.reshape(b * s, d_rnn),
                    p["w_attn_out"], p["w_rnn_out"], z, p["b_gate"], c_g)
    x1 = _outproj(merged, p["w_out"], x2)
    hm = _rmsnorm_cast(x1, p["norm_mlp"])
    up = _mlp_up(hm, p["w_up"])
    return up, x1, normed


def _trunks(x_a, x_b, layers, norm_final):
    (b_a, s_a, d), (b_b, s_b, _) = x_a.shape, x_b.shape
    xa, xb = x_a.reshape(b_a * s_a, d), x_b.reshape(b_b * s_b, d)
    h_b = None
    for li, p in enumerate(layers):
        norms = ((xb, p["norm_mix"], BF16),) if li == 0 else ()
        up, x1, normed = _encoder_layer(xa, None, b_a, s_a, p, norms)
        if li == 0:
            h_b = normed[0]
        xa = _mlp_down(up, p["w_down"], x1)
    y_a = None
    for li, p in enumerate(layers):
        last = li == len(layers) - 1
        norms = ((xa, norm_final, F32),) if last else ()
        up, x1, normed = _encoder_layer(xb, h_b if li == 0 else None, b_b, s_b, p, norms)
        if last:
            y_a = normed[0]
        xb = _mlp_down(up, p["w_down"], x1)
    y_b = _rmsnorm_cast(xb, norm_final, out_dtype=F32)
    return y_a.reshape(b_a, s_a, d), y_b.reshape(b_b, s_b, d)


def kernel(x_prompt, x_sample, norm_mix, w_in, q_norm, k_norm, conv_w, conv_b, lru_w_a, lru_b_a,
           lru_w_i, lru_b_i, lru_lambda, w_attn_out, w_rnn_out, b_gate, w_out, norm_mlp, w_up,
           w_down, norm_final):
    depth = w_in.shape[0]
    layers = []
    for l in range(depth):
        layers.append(dict(
            norm_mix=norm_mix[l], w_in=w_in[l].astype(BF16), q_norm=q_norm[l], k_norm=k_norm[l],
            conv_w=conv_w[l], conv_b=conv_b[l], lru_w_a=lru_w_a[l], lru_b_a=lru_b_a[l],
            lru_w_i=lru_w_i[l], lru_b_i=lru_b_i[l], lru_lambda=lru_lambda[l],
            w_attn_out=w_attn_out[l], w_rnn_out=w_rnn_out[l], b_gate=b_gate[l], w_out=w_out[l],
            norm_mlp=norm_mlp[l], w_up=w_up[l], w_down=w_down[l]))
    return _trunks(x_prompt, x_sample, layers, norm_final)
```

```python
import functools
import math

import jax
import jax.numpy as jnp
from jax import lax
from jax.experimental import pallas as pl
from jax.experimental.pallas import tpu as pltpu

F32 = jnp.float32
BF16 = jnp.bfloat16

HEAD_DIM = 128
N_Q_HEADS = 16
N_KV_HEADS = 4
GQA_GROUP = N_Q_HEADS // N_KV_HEADS
ATTN_WIDTH = N_Q_HEADS * HEAD_DIM
KV_WIDTH = N_KV_HEADS * HEAD_DIM
ROPE_THETA = 10000.0
GRID_W = 64
LRU_C = 8.0
CONV_W = 4
NORM_EPS = 1e-6
F32_TINY = 1.1754944e-38
LANES = 128
HALO = 16
V7X_VMEM_LIMIT = 60 * 1024 * 1024


def _params(n_axes, vmem=V7X_VMEM_LIMIT):
    return pltpu.CompilerParams(
        dimension_semantics=("arbitrary",) * n_axes, vmem_limit_bytes=vmem)


def _tile(n, want):
    t = min(n, want)
    while n % t:
        t //= 2
    return t


def _rmsnorm_cast_kernel(x_ref, g_ref, o_ref):
    x = x_ref[...]
    ms = jnp.mean(x * x, axis=-1, keepdims=True)
    o_ref[...] = (x * lax.rsqrt(ms + NORM_EPS) * g_ref[...]).astype(o_ref.dtype)


def _rmsnorm_cast(x, g, out_dtype=BF16):
    t, d = x.shape
    tr = _tile(t, 256)
    return pl.pallas_call(
        _rmsnorm_cast_kernel,
        grid=(t // tr,),
        in_specs=[pl.BlockSpec((tr, d), lambda i: (i, 0)),
                  pl.BlockSpec((1, d), lambda i: (0, 0))],
        out_specs=pl.BlockSpec((tr, d), lambda i: (i, 0)),
        out_shape=jax.ShapeDtypeStruct((t, d), out_dtype),
        compiler_params=_params(1),
        name="rmsnorm_cast",
    )(x, g.reshape(1, d))


def _mlp_up_kernel(a_ref, w_ref, o_ref):
    y = jnp.dot(a_ref[...], w_ref[...], preferred_element_type=F32)
    r = jnp.maximum(y, 0.0)
    o_ref[...] = (r * r).astype(o_ref.dtype)


def _mlp_up(a, w):
    m, k = a.shape
    n = w.shape[1]
    tm, tn = _tile(m, 1024), _tile(n, 1024)
    return pl.pallas_call(
        _mlp_up_kernel,
        grid=(m // tm, n // tn),
        in_specs=[pl.BlockSpec((tm, k), lambda i, j: (i, 0)),
                  pl.BlockSpec((k, tn), lambda i, j: (0, j))],
        out_specs=pl.BlockSpec((tm, tn), lambda i, j: (i, j)),
        out_shape=jax.ShapeDtypeStruct((m, n), BF16),
        compiler_params=_params(2),
        name="mlp_up",
    )(a, w)


def _merge_kernel(attn_ref, rec_ref, wa_ref, wr_ref, ga_ref, gr_ref, ba_ref, br_ref, o_ref):
    ab = jnp.dot(attn_ref[...], wa_ref[...], preferred_element_type=F32)
    rb = jnp.dot(rec_ref[...], wr_ref[...], preferred_element_type=F32)
    ga = jax.nn.sigmoid(ga_ref[...].astype(F32) + ba_ref[...])
    gr = jax.nn.sigmoid(gr_ref[...].astype(F32) + br_ref[...])
    o_ref[...] = (ga * ab + gr * rb).astype(o_ref.dtype)


def _merge(attn, rec, wa, wr, z, b_gate, gate_col0):
    m, ka = attn.shape
    kr = rec.shape[1]
    n = wa.shape[1]
    tm, tn = _tile(m, 1024), _tile(n, 1024)
    ga_blk = gate_col0 // tn
    gr_blk = (gate_col0 + n) // tn
    return pl.pallas_call(
        _merge_kernel,
        grid=(m // tm, n // tn),
        in_specs=[pl.BlockSpec((tm, ka), lambda i, j: (i, 0)),
                  pl.BlockSpec((tm, kr), lambda i, j: (i, 0)),
                  pl.BlockSpec((ka, tn), lambda i, j: (0, j)),
                  pl.BlockSpec((kr, tn), lambda i, j: (0, j)),
                  pl.BlockSpec((tm, tn), lambda i, j: (i, ga_blk + j)),
                  pl.BlockSpec((tm, tn), lambda i, j: (i, gr_blk + j)),
                  pl.BlockSpec((1, tn), lambda i, j: (0, j)),
                  pl.BlockSpec((1, tn), lambda i, j: (0, j))],
        out_specs=pl.BlockSpec((tm, tn), lambda i, j: (i, j)),
        out_shape=jax.ShapeDtypeStruct((m, n), BF16),
        compiler_params=_params(2),
        name="branch_merge",
    )(attn, rec, wa, wr, z, z, b_gate[0:1], b_gate[1:2])


def _outproj_kernel(a_ref, w_ref, x_ref, o_ref):
    o_ref[...] = x_ref[...] + jnp.dot(a_ref[...], w_ref[...], preferred_element_type=F32)


def _outproj(a, w, x):
    m, k = a.shape
    n = w.shape[1]
    tm, tn = _tile(m, 1024), _tile(n, 1024)
    return pl.pallas_call(
        _outproj_kernel,
        grid=(m // tm, n // tn),
        in_specs=[pl.BlockSpec((tm, k), lambda i, j: (i, 0)),
                  pl.BlockSpec((k, tn), lambda i, j: (0, j)),
                  pl.BlockSpec((tm, tn), lambda i, j: (i, j))],
        out_specs=pl.BlockSpec((tm, tn), lambda i, j: (i, j)),
        out_shape=jax.ShapeDtypeStruct((m, n), F32),
        compiler_params=_params(2),
        name="out_proj",
    )(a, w, x)


def _mlp_down_kernel(u_ref, w_ref, x_ref, o_ref):
    def product():
        return jnp.dot(u_ref[...], w_ref[...], preferred_element_type=F32)

    @pl.when(pl.program_id(2) == 0)
    def _():
        o_ref[...] = x_ref[...] + product()

    @pl.when(pl.program_id(2) != 0)
    def _():
        o_ref[...] += product()


def _mlp_down(u, w, x):
    m, k = u.shape
    n = w.shape[1]
    tm, tn, tk = _tile(m, 1024), _tile(n, 1024), _tile(k, 4096)
    return pl.pallas_call(
        _mlp_down_kernel,
        grid=(m // tm, n // tn, k // tk),
        in_specs=[pl.BlockSpec((tm, tk), lambda i, j, kk: (i, kk)),
                  pl.BlockSpec((tk, tn), lambda i, j, kk: (kk, j)),
                  pl.BlockSpec((tm, tn), lambda i, j, kk: (i, j))],
        out_specs=pl.BlockSpec((tm, tn), lambda i, j, kk: (i, j)),
        out_shape=jax.ShapeDtypeStruct((m, n), F32),
        compiler_params=_params(3),
        name="mlp_down",
    )(u, w, x)


Q_SCALE = math.log2(math.e) * HEAD_DIM ** -0.5


def _rope_tables(seq_len):
    axis_dim = HEAD_DIM // 2
    n_rows = seq_len // GRID_W
    row = jnp.repeat(jnp.arange(n_rows), GRID_W).astype(F32)
    col = jnp.tile(jnp.arange(GRID_W), n_rows).astype(F32)
    inv = ROPE_THETA ** (-jnp.arange(0, axis_dim, 2, dtype=F32) / axis_dim)
    ang = jnp.concatenate([row[:, None] * inv, col[:, None] * inv], axis=-1)
    cos, sin = jnp.cos(ang), jnp.sin(ang)
    cosf = jnp.repeat(cos, 2, axis=-1)
    sinf = jnp.stack([-sin, sin], axis=-1).reshape(seq_len, HEAD_DIM)
    return cosf, sinf


IN_PROJ_TN = 2 * KV_WIDTH


def _norm_rope(x, g, cos, sin):
    ms = jnp.mean(x * x, axis=-1, keepdims=True)
    xn = x * lax.rsqrt(ms + NORM_EPS) * g
    even = (lax.broadcasted_iota(jnp.int32, xn.shape, 1) & 1) == 0
    nxt = pltpu.roll(xn, HEAD_DIM - 1, 1)
    prv = pltpu.roll(xn, 1, 1)
    return xn * cos + jnp.where(even, nxt, prv) * sin


def _in_proj_kernel(a_ref, w_ref, cos_ref, sin_ref, kg_ref, ko_ref, vt_ref, z_ref, *, kv_tile):
    j = pl.program_id(1)

    def product():
        return jnp.dot(a_ref[...], w_ref[...], preferred_element_type=F32)

    @pl.when(j == kv_tile)
    def _():
        y = product()
        for h in range(N_KV_HEADS):
            sl = slice(h * HEAD_DIM, (h + 1) * HEAD_DIM)
            ko_ref[:, sl] = _norm_rope(y[:, sl], kg_ref[...], cos_ref[...],
                                       sin_ref[...]).astype(ko_ref.dtype)
            vsl = slice(KV_WIDTH + h * HEAD_DIM, KV_WIDTH + (h + 1) * HEAD_DIM)
            vt_ref[sl, :] = y[:, vsl].T.astype(vt_ref.dtype)

    @pl.when(j != kv_tile)
    def _():
        z_ref[...] = product().astype(z_ref.dtype)


def _in_proj(h, w, k_norm, cosf, sinf, b, s):
    t, k = h.shape
    n = w.shape[1]
    tn = IN_PROJ_TN
    tm = _tile(s, 1024)
    assert ATTN_WIDTH % tn == 0 and n % tn == 0
    kv_tile = ATTN_WIDTH // tn
    tps = s // tm

    def z_block(i, j):
        return i, jnp.where(j < kv_tile, j, j - 1)

    return pl.pallas_call(
        functools.partial(_in_proj_kernel, kv_tile=kv_tile),
        grid=(t // tm, n // tn),
        in_specs=[pl.BlockSpec((tm, k), lambda i, j: (i, 0)),
                  pl.BlockSpec((k, tn), lambda i, j: (0, j)),
                  pl.BlockSpec((tm, HEAD_DIM), lambda i, j: (i % tps, 0)),
                  pl.BlockSpec((tm, HEAD_DIM), lambda i, j: (i % tps, 0)),
                  pl.BlockSpec((1, HEAD_DIM), lambda i, j: (0, 0))],
        out_specs=[pl.BlockSpec((None, tm, KV_WIDTH), lambda i, j: (i // tps, i % tps, 0)),
                   pl.BlockSpec((None, KV_WIDTH, tm), lambda i, j: (i // tps, 0, i % tps)),
                   pl.BlockSpec((tm, tn), z_block)],
        out_shape=[jax.ShapeDtypeStruct((b, s, KV_WIDTH), BF16),
                   jax.ShapeDtypeStruct((b, KV_WIDTH, s), BF16),
                   jax.ShapeDtypeStruct((t, n - tn), BF16)],
        compiler_params=_params(2),
        name="in_proj",
    )(h, w, cosf, sinf, k_norm.reshape(1, HEAD_DIM))


MAX_STATIC_PAIRS = 8


N_ATTN_INPUTS = 9


def _attn_kernel(*refs, tk, cast_blocks, norm_blocks, n_steps):
    n_cast, n_norm = len(cast_blocks), len(norm_blocks)
    (q_ref, qn_ref, cos_ref, sin_ref, cosn_ref, sinn_ref, qg_ref, k_ref,
     vt_ref) = refs[:N_ATTN_INPUTS]
    pos = N_ATTN_INPUTS
    cast_in = refs[pos:pos + n_cast]
    pos += n_cast
    norm_in = refs[pos:pos + 2 * n_norm]
    pos += 2 * n_norm
    o_ref = refs[pos]
    cast_out = refs[pos + 1:pos + 1 + n_cast]
    norm_out = refs[pos + 1 + n_cast:pos + 1 + n_cast + n_norm]
    qt_scr, st_scr, p_scr, acc_scr = refs[pos + 1 + n_cast + n_norm:]
    tq = q_ref.shape[0]
    n_pairs = k_ref.shape[0] // (2 * tk)
    n_chunks = 2 * n_pairs

    g_step = ((pl.program_id(0) * pl.num_programs(1) + pl.program_id(1)) * pl.num_programs(2)
              + pl.program_id(2))
    every_step = []

    def side_job(n_blocks, body):
        if n_blocks == n_steps:
            every_step.append(body)
        else:
            pl.when(g_step < n_blocks)(body)

    for src, dst, n_blocks in zip(cast_in, cast_out, cast_blocks):
        def cast_block(src=src, dst=dst):
            dst[...] = src[...].astype(dst.dtype)
        side_job(n_blocks, cast_block)
    for j, (dst, n_blocks) in enumerate(zip(norm_out, norm_blocks)):
        def norm_block(x_ref=norm_in[2 * j], g_ref=norm_in[2 * j + 1], dst=dst):
            _rmsnorm_cast_kernel(x_ref, g_ref, dst)
        side_job(n_blocks, norm_block)

    def stage_queries(src_ref, c_ref, s_ref):
        for g in range(GQA_GROUP):
            x = src_ref[:, g * HEAD_DIM:(g + 1) * HEAD_DIM].astype(F32)
            q = _norm_rope(x, qg_ref[...], c_ref[...], s_ref[...]) * Q_SCALE
            qt_scr[:, g * tq:(g + 1) * tq] = q.T.astype(qt_scr.dtype)

    def scores(c, slot):
        start = pl.multiple_of(c * tk, tk)
        st_scr[slot] = jnp.dot(k_ref[pl.ds(start, tk), :], qt_scr[...],
                               preferred_element_type=F32)

    def weighted_values(c, slot, alpha):
        start = pl.multiple_of(c * tk, tk)
        acc_scr[...] = alpha * acc_scr[...] + jnp.dot(
            vt_ref[:, pl.ds(start, tk)], p_scr[slot], preferred_element_type=F32)

    def softmax(slot, m_old, l_old):
        st = st_scr[slot]
        m_new = jnp.maximum(m_old, jnp.max(st, axis=0, keepdims=True))
        alpha = jnp.exp2(m_old - m_new)
        p = jnp.exp2(st - m_new)
        p_scr[slot] = p.astype(p_scr.dtype)
        return m_new, alpha * l_old + jnp.sum(p, axis=0, keepdims=True), alpha

    def step(c, slot, carry, has_prev):
        m, l, alpha_prev = carry
        if has_prev:
            weighted_values(c - 1, 1 - slot, alpha_prev)
        if isinstance(c, int) and c == n_chunks - 1:
            stage_queries(qn_ref, cosn_ref, sinn_ref)
            scores(0, 0)
        else:
            scores(c + 1, 1 - slot)
        return softmax(slot, m, l)

    def pair(c, carry, first):
        carry = step(c, 0, carry, not first)
        return step(c + 1, 1, carry, True)

    @pl.when(pl.program_id(2) == 0)
    def _():
        stage_queries(q_ref, cos_ref, sin_ref)
        scores(0, 0)

    for body in every_step:
        body()
    acc_scr[...] = jnp.zeros(acc_scr.shape, F32)
    rows = acc_scr.shape[1]
    zero = jnp.zeros((1, rows), F32)
    carry = (jnp.full((1, rows), -jnp.inf, F32), zero, zero)
    if n_pairs <= MAX_STATIC_PAIRS:
        for j in range(n_pairs):
            carry = pair(2 * j, carry, j == 0)
    else:
        carry = pair(0, carry, True)
        carry = lax.fori_loop(1, n_pairs - 1, lambda j, cr: pair(2 * j, cr, False), carry)
        carry = pair(n_chunks - 2, carry, False)
    _, l_fin, alpha_last = carry
    weighted_values(n_chunks - 1, 1, alpha_last)
    out = acc_scr[...] * (1.0 / l_fin)
    for g in range(GQA_GROUP):
        o_ref[:, g * HEAD_DIM:(g + 1) * HEAD_DIM] = out[:, g * tq:(g + 1) * tq].T.astype(o_ref.dtype)


CAST_BLOCK = (256, 1024)


def _attention(z3, q_norm, cosf, sinf, kr, vt, cast=(), norms=()):
    b, s, _ = kr.shape
    tq = _tile(s, 256)
    tk = _tile(s // 2, 512)
    rows = GQA_GROUP * tq
    gw = GQA_GROUP * HEAD_DIM
    n_tiles = s // tq
    n_steps = b * N_KV_HEADS * n_tiles

    def nxt(i):
        return jnp.minimum(i + 1, n_tiles - 1)

    cast_specs, cast_shapes, cast_blocks = [], [], []
    for w in cast:
        br, bc = _tile(w.shape[0], CAST_BLOCK[0]), _tile(w.shape[1], CAST_BLOCK[1])
        while (w.shape[0] // br) * (w.shape[1] // bc) > n_steps:
            br *= 2
        n_bc = w.shape[1] // bc
        n_blocks = (w.shape[0] // br) * n_bc

        def block(bi, h, i, n_bc=n_bc, n_blocks=n_blocks):
            g = jnp.minimum((bi * N_KV_HEADS + h) * n_tiles + i, n_blocks - 1)
            return g // n_bc, g % n_bc

        cast_specs.append(pl.BlockSpec((br, bc), block))
        cast_shapes.append(jax.ShapeDtypeStruct(w.shape, BF16))
        cast_blocks.append(n_blocks)

    norm_in_specs, norm_out_specs, norm_shapes, norm_blocks, norm_args = [], [], [], [], []
    for x, g, dtype in norms:
        t, d = x.shape
        br = _tile(t, max(16, t // n_steps))
        assert br * n_steps >= t
        n_blocks = t // br

        def row_block(bi, h, i, n_blocks=n_blocks):
            return jnp.minimum((bi * N_KV_HEADS + h) * n_tiles + i, n_blocks - 1), 0

        norm_in_specs += [pl.BlockSpec((br, d), row_block),
                          pl.BlockSpec((1, d), lambda bi, h, i: (0, 0))]
        norm_out_specs.append(pl.BlockSpec((br, d), row_block))
        norm_shapes.append(jax.ShapeDtypeStruct((t, d), dtype))
        norm_blocks.append(n_blocks)
        norm_args += [x, g.reshape(1, d)]

    outs = pl.pallas_call(
        functools.partial(_attn_kernel, tk=tk, cast_blocks=tuple(cast_blocks),
                          norm_blocks=tuple(norm_blocks), n_steps=n_steps),
        grid=(b, N_KV_HEADS, n_tiles),
        in_specs=[pl.BlockSpec((None, tq, gw), lambda bi, h, i: (bi, i, h)),
                  pl.BlockSpec((None, tq, gw), lambda bi, h, i: (bi, nxt(i), h)),
                  pl.BlockSpec((tq, HEAD_DIM), lambda bi, h, i: (i, 0)),
                  pl.BlockSpec((tq, HEAD_DIM), lambda bi, h, i: (i, 0)),
                  pl.BlockSpec((tq, HEAD_DIM), lambda bi, h, i: (nxt(i), 0)),
                  pl.BlockSpec((tq, HEAD_DIM), lambda bi, h, i: (nxt(i), 0)),
                  pl.BlockSpec((1, HEAD_DIM), lambda bi, h, i: (0, 0)),
                  pl.BlockSpec((None, s, HEAD_DIM), lambda bi, h, i: (bi, 0, h)),
                  pl.BlockSpec((None, HEAD_DIM, s), lambda bi, h, i: (bi, h, 0))]
        + cast_specs + norm_in_specs,
        out_specs=[pl.BlockSpec((None, tq, gw), lambda bi, h, i: (bi, i, h))]
        + cast_specs + norm_out_specs,
        out_shape=[jax.ShapeDtypeStruct((b, s, ATTN_WIDTH), BF16)] + cast_shapes + norm_shapes,
        scratch_shapes=[pltpu.VMEM((HEAD_DIM, rows), BF16),
                        pltpu.VMEM((2, tk, rows), F32),
                        pltpu.VMEM((2, tk, rows), BF16),
                        pltpu.VMEM((HEAD_DIM, rows), F32)],
        compiler_params=_params(3),
        name="gqa_attention",
    )(z3, z3, cosf, sinf, cosf, sinf, q_norm.reshape(1, HEAD_DIM), kr, vt, *cast, *norm_args)
    n_cast = len(cast)
    return outs[0], tuple(outs[1:1 + n_cast]), tuple(outs[1 + n_cast:])


def _softplus(x):
    return jnp.maximum(x, 0.0) + jnp.log1p(jnp.exp(-jnp.abs(x)))


def _gelu_tanh(x):
    c = math.sqrt(2.0 / math.pi)
    return 0.5 * x * (1.0 + jnp.tanh(c * (x + 0.044715 * (x * x * x))))


def _lru_kernel(xr_ref, yr_ref, cw_ref, cb_ref, wg_ref, bg_ref, lam_ref, o_ref,
                xpad, hf, hb, *, chunk):
    s = xr_ref.shape[0]
    n_chunks = s // chunk
    n_tiles = chunk // 8
    w = LANES

    xpad[0:HALO, :] = jnp.zeros((HALO, w), F32)
    xpad[s + HALO:s + 2 * HALO, :] = jnp.zeros((HALO, w), F32)

    def fill(c, carry):
        t0 = pl.multiple_of(c * chunk, chunk)
        xpad[pl.ds(t0 + HALO, chunk), :] = xr_ref[pl.ds(t0, chunk), :].astype(F32)
        return carry

    lax.fori_loop(0, n_chunks, fill, 0)

    cw = cw_ref[...]
    cb = cb_ref[...]
    neg_c_sp = -LRU_C * _softplus(-lam_ref[...])
    sub = lax.broadcasted_iota(jnp.int32, (1, 8, w), 1)

    def conv(t0):
        out = cb
        for j in range(CONV_W):
            out = out + cw[j:j + 1] * xpad[pl.ds(t0 + (HALO - 1 + j), chunk), :]
        return out

    def gate_terms(xc, direction):
        col = 2 * w * direction
        g = jnp.dot(xc.astype(BF16), wg_ref[:, col:col + 2 * w],
                    preferred_element_type=F32) + bg_ref[:, col:col + 2 * w]
        r = jax.nn.sigmoid(g[:, :w])
        i = jax.nn.sigmoid(g[:, w:])
        log_a = r * neg_c_sp[direction:direction + 1]
        a = jnp.exp(log_a)
        m = -jnp.tanh(log_a) * (a * a + 1.0)
        root = m * lax.rsqrt(jnp.maximum(m, F32_TINY))
        u = root * (i * xc)
        return a.reshape(n_tiles, 8, w), u.reshape(n_tiles, 8, w)

    def tile_scan(a, u, reverse):
        for d in (1, 2, 4):
            shift = (8 - d) if reverse else d
            keep = (sub < 8 - d) if reverse else (sub >= d)
            a_sh = pltpu.roll(a, shift, 1)
            u_sh = pltpu.roll(u, shift, 1)
            u = jnp.where(keep, u + a * u_sh, u)
            a = jnp.where(keep, a * a_sh, a)
        return a, u

    def body(c, carry):
        h_f, h_b = carry
        tf = pl.multiple_of(c * chunk, chunk)
        tb = pl.multiple_of((n_chunks - 1 - c) * chunk, chunk)
        a_f, u_f = tile_scan(*gate_terms(conv(tf), 0), reverse=False)
        a_b, u_b = tile_scan(*gate_terms(conv(tb), 1), reverse=True)
        for k in range(n_tiles):
            ht = u_f[k] + a_f[k] * h_f
            hf[pl.ds(tf + 8 * k, 8), :] = ht
            h_f = jnp.broadcast_to(ht[7:8, :], (8, w))
            kb = n_tiles - 1 - k
            ht = u_b[kb] + a_b[kb] * h_b
            hb[pl.ds(tb + 8 * kb, 8), :] = ht
            h_b = jnp.broadcast_to(ht[0:1, :], (8, w))
        return h_f, h_b

    zero = jnp.zeros((8, w), F32)
    lax.fori_loop(0, n_chunks, body, (zero, zero))

    def emit(c, carry):
        t0 = pl.multiple_of(c * chunk, chunk)
        y = yr_ref[pl.ds(t0, chunk), :].astype(F32)
        rec = (hf[pl.ds(t0, chunk), :] + hb[pl.ds(t0, chunk), :]) * _gelu_tanh(y)
        o_ref[pl.ds(t0, chunk), :] = rec.astype(o_ref.dtype)
        return carry

    lax.fori_loop(0, n_chunks, emit, 0)


def _lru(z3, conv_w, conv_b, w_a, b_a, w_i, b_i, lam, x_col0, y_col0):
    b, s, _ = z3.shape
    n_blocks = w_a.shape[1]
    d_rnn = n_blocks * LANES
    chunk = _tile(s, 2048)
    wg = jnp.concatenate([w_a[0], w_i[0], w_a[1], w_i[1]], axis=-1).astype(BF16)
    bg = jnp.concatenate(
        [v.reshape(n_blocks, 1, LANES) for v in (b_a[0], b_i[0], b_a[1], b_i[1])], axis=-1)
    x_blk = x_col0 // LANES
    y_blk = y_col0 // LANES
    return pl.pallas_call(
        functools.partial(_lru_kernel, chunk=chunk),
        grid=(b, n_blocks),
        in_specs=[pl.BlockSpec((None, s, LANES), lambda bi, j: (bi, 0, x_blk + j)),
                  pl.BlockSpec((None, s, LANES), lambda bi, j: (bi, 0, y_blk + j)),
                  pl.BlockSpec((CONV_W, LANES), lambda bi, j: (0, j)),
                  pl.BlockSpec((1, LANES), lambda bi, j: (0, j)),
                  pl.BlockSpec((None, LANES, 4 * LANES), lambda bi, j: (j, 0, 0)),
                  pl.BlockSpec((None, 1, 4 * LANES), lambda bi, j: (j, 0, 0)),
                  pl.BlockSpec((2, LANES), lambda bi, j: (0, j))],
        out_specs=pl.BlockSpec((None, s, LANES), lambda bi, j: (bi, 0, j)),
        out_shape=jax.ShapeDtypeStruct((b, s, d_rnn), BF16),
        scratch_shapes=[pltpu.VMEM((s + 2 * HALO, LANES), F32),
                        pltpu.VMEM((s, LANES), F32),
                        pltpu.VMEM((s, LANES), F32)],
        compiler_params=_params(2),
        name="conv_rglru",
    )(z3, z3, conv_w, conv_b.reshape(1, d_rnn), wg, bg, lam)


LATE_WEIGHTS = ("w_attn_out", "w_rnn_out", "w_out", "w_up", "w_down")


def _encoder_layer(x2, h, b, s, p, norms=()):
    d_rnn = p["w_rnn_out"].shape[0]
    c_x = ATTN_WIDTH
    c_y = c_x + d_rnn
    c_g = c_y + d_rnn

    if h is None:
        h = _rmsnorm_cast(x2, p["norm_mix"])
    cosf, sinf = _rope_tables(s)
    kr, vt, z = _in_proj(h, p["w_in"], p["k_norm"], cosf, sinf, b, s)
    z3 = z.reshape(b, s, z.shape[1])
    pending = tuple(k for k in LATE_WEIGHTS if p[k].dtype != BF16)
    attn, converted, normed = _attention(z3, p["q_norm"], cosf, sinf, kr, vt,
                                         cast=tuple(p[k] for k in pending), norms=norms)
    p.update(zip(pending, converted))
    rec = _lru(z3, p["conv_w"], p["conv_b"], p["lru_w_a"], p["lru_b_a"], p["lru_w_i"],
               p["lru_b_i"], p["lru_lambda"], c_x, c_y)

    merged = _merge(attn.reshape(b * s, ATTN_WIDTH), rec.reshape(b * s, d_rnn),
                    p["w_attn_out"], p["w_rnn_out"], z, p["b_gate"], c_g)
    x1 = _outproj(merged, p["w_out"], x2)
    hm = _rmsnorm_cast(x1, p["norm_mlp"])
    up = _mlp_up(hm, p["w_up"])
    return up, x1, normed


def _trunks(x_a, x_b, layers, norm_final):
    (b_a, s_a, d), (b_b, s_b, _) = x_a.shape, x_b.shape
    xa, xb = x_a.reshape(b_a * s_a, d), x_b.reshape(b_b * s_b, d)
    h_b = None
    for li, p in enumerate(layers):
        norms = ((xb, p["norm_mix"], BF16),) if li == 0 else ()
        up, x1, normed = _encoder_layer(xa, None, b_a, s_a, p, norms)
        if li == 0:
            h_b = normed[0]
        xa = _mlp_down(up, p["w_down"], x1)
    y_a = None
    for li, p in enumerate(layers):
        last = li == len(layers) - 1
        norms = ((xa, norm_final, F32),) if last else ()
        up, x1, normed = _encoder_layer(xb, h_b if li == 0 else None, b_b, s_b, p, norms)
        if last:
            y_a = normed[0]
        xb = _mlp_down(up, p["w_down"], x1)
    y_b = _rmsnorm_cast(xb, norm_final, out_dtype=F32)
    return y_a.reshape(b_a, s_a, d), y_b.reshape(b_b, s_b, d)


def kernel(x_prompt, x_sample, norm_mix, w_in, q_norm, k_norm, conv_w, conv_b, lru_w_a, lru_b_a,
           lru_w_i, lru_b_i, lru_lambda, w_attn_out, w_rnn_out, b_gate, w_out, norm_mlp, w_up,
           w_down, norm_final):
    depth = w_in.shape[0]
    layers = []
    for l in range(depth):
        layers.append(dict(
            norm_mix=norm_mix[l], w_in=w_in[l].astype(BF16), q_norm=q_norm[l], k_norm=k_norm[l],
            conv_w=conv_w[l], conv_b=conv_b[l], lru_w_a=lru_w_a[l], lru_b_a=lru_b_a[l],
            lru_w_i=lru_w_i[l], lru_b_i=lru_b_i[l], lru_lambda=lru_lambda[l],
            w_attn_out=w_attn_out[l], w_rnn_out=w_rnn_out[l], b_gate=b_gate[l], w_out=w_out[l],
            norm_mlp=norm_mlp[l], w_up=w_up[l], w_down=w_down[l]))
    return _trunks(x_prompt, x_sample, layers, norm_final)
```
